```python
import math
import jax, jax.numpy as jnp
from jax import lax
import numpy as np

D_MODEL = 2048
BATCH = 4
SEQ = 4096
DEPTH = 1
DEC_BATCH = 8
DEC_SEQ = 64
PAST_LEN = 1024

CHUNK = 64
N_META = 16
Q_BLOCK = 128
EPS = 1e-6
N_DIFF_HEADS = 8
DIFF_HEAD_DIM = 64
DIFF_V_DIM = 2 * DIFF_HEAD_DIM
D_DIFF_QK = N_DIFF_HEADS * 2 * DIFF_HEAD_DIM
D_DIFF = N_DIFF_HEADS * DIFF_V_DIM
DIFF_SCALE = DIFF_HEAD_DIM ** -0.5
N_MLA_HEADS = 8
MLA_NOPE_DIM = 128
MLA_ROPE_DIM = 64
MLA_V_DIM = 128
MLA_Q_LORA = 512
MLA_KV_LORA = 256
D_MLA = N_MLA_HEADS * MLA_V_DIM
MLA_SCALE = (MLA_NOPE_DIM + MLA_ROPE_DIM) ** -0.5
ROPE_THETA = 10000.0
D_MIX = D_DIFF + D_MLA
D_IN = 2 * D_DIFF_QK + D_DIFF + MLA_Q_LORA + MLA_KV_LORA + MLA_ROPE_DIM
OFF_K = D_DIFF_QK
OFF_V = 2 * D_DIFF_QK
OFF_CQ = OFF_V + D_DIFF
OFF_CKV = OFF_CQ + MLA_Q_LORA
OFF_KR = OFF_CKV + MLA_KV_LORA
REL_BUCKETS = 32
REL_MAX_DIST = 128
D_FF = ((8 * D_MODEL // 3 + 255) // 256) * 256

kernel_name = 'hymba_diff_mla_streaming_encoder_step'


def rmsnorm(x, g):
    xf = x.astype(jnp.float32)
    y = xf * lax.rsqrt(jnp.mean(xf * xf, axis=-1, keepdims=True) + EPS)
    return (y * g.astype(jnp.float32)).astype(x.dtype)


def rope(x, pos):
    half = x.shape[-1] // 2
    inv_freq = ROPE_THETA ** (-jnp.arange(half, dtype=jnp.float32) / half)
    ang = pos.astype(jnp.float32)[:, None] * inv_freq[None, :]
    shape = (1, pos.shape[0]) + (1,) * (x.ndim - 3) + (half,)
    cos = jnp.cos(ang).reshape(shape)
    sin = jnp.sin(ang).reshape(shape)
    xf = x.astype(jnp.float32)
    x1, x2 = xf[..., :half], xf[..., half:]
    return jnp.concatenate([x1 * cos - x2 * sin, x1 * sin + x2 * cos], axis=-1).astype(x.dtype)


def t5_bucket(rel):
    nb = REL_BUCKETS // 2
    max_exact = nb // 2
    ret = jnp.where(rel > 0, nb, 0)
    n = jnp.abs(rel)
    large = max_exact + (jnp.log(jnp.maximum(n, 1).astype(jnp.float32) / max_exact)
                         / math.log(REL_MAX_DIST / max_exact) * (nb - max_exact)).astype(jnp.int32)
    large = jnp.minimum(large, nb - 1)
    return ret + jnp.where(n < max_exact, n, large)


def project(h, pos, lp):
    B, L, _ = h.shape
    z = h @ lp['w_in']
    qd = z[..., :OFF_K].reshape(B, L, N_DIFF_HEADS, 2, DIFF_HEAD_DIM)
    kd = z[..., OFF_K:OFF_V].reshape(B, L, N_DIFF_HEADS, 2, DIFF_HEAD_DIM)
    vd = z[..., OFF_V:OFF_CQ].reshape(B, L, N_DIFF_HEADS, DIFF_V_DIM)
    cq = rmsnorm(z[..., OFF_CQ:OFF_CKV], lp['mla_q_norm_g'])
    qm = (cq @ lp['mla_w_uq']).reshape(B, L, N_MLA_HEADS, MLA_NOPE_DIM + MLA_ROPE_DIM)
    qm = jnp.concatenate([qm[..., :MLA_NOPE_DIM], rope(qm[..., MLA_NOPE_DIM:], pos)], axis=-1)
    ckv = rmsnorm(z[..., OFF_CKV:OFF_KR], lp['mla_kv_norm_g'])
    krope = rope(z[..., OFF_KR:], pos)
    return qd, kd, vd, qm, ckv, krope


def expand_latent(ckv, w_ukv):
    B, K, _ = ckv.shape
    kv = (ckv @ w_ukv).reshape(B, K, N_MLA_HEADS, MLA_NOPE_DIM + MLA_V_DIM)
    return kv[..., :MLA_NOPE_DIM], kv[..., MLA_NOPE_DIM:]


def attend_block(qd, qm, qpos, qchunk, kd, vd, km, krope, vm, kpos, kchunk, rel_bias, lam):
    neg = jnp.finfo(jnp.float32).min
    visible = kchunk[None, :] <= qchunk[:, None]
    bias = jnp.moveaxis(rel_bias[t5_bucket(kpos[None, :] - qpos[:, None])], -1, 0).astype(jnp.float32)
    s = jnp.einsum('bqhcd,bkhcd->bhcqk', qd, kd).astype(jnp.float32) * DIFF_SCALE + bias[None, :, None]
    p = jax.nn.softmax(jnp.where(visible, s, neg), axis=-1)
    p = p[:, :, 0] - lam * p[:, :, 1]
    o_d = jnp.einsum('bhqk,bkhe->bqhe', p.astype(vd.dtype), vd)
    qn, qr = qm[..., :MLA_NOPE_DIM], qm[..., MLA_NOPE_DIM:]
    sm = (jnp.einsum('bqhd,bkhd->bhqk', qn, km)
          + jnp.einsum('bqhr,bkr->bhqk', qr, krope)).astype(jnp.float32) * MLA_SCALE
    pm = jax.nn.softmax(jnp.where(visible, sm, neg), axis=-1)
    o_m = jnp.einsum('bhqk,bkhe->bqhe', pm.astype(vm.dtype), vm)
    return o_d, o_m


def blocked_attention(qd, qm, qpos, qchunk, kd, vd, km, krope, vm, kpos, kchunk, rel_bias, lam):
    L = qd.shape[1]
    nb = -(-L // Q_BLOCK)
    pad = nb * Q_BLOCK - L

    def to_blocks(a):
        a = jnp.pad(a, [(0, 0), (0, pad)] + [(0, 0)] * (a.ndim - 2))
        a = a.reshape((a.shape[0], nb, Q_BLOCK) + a.shape[2:])
        return jnp.moveaxis(a, 1, 0)

    def from_blocks(a):
        a = jnp.moveaxis(a, 0, 1)
        return a.reshape((a.shape[0], nb * Q_BLOCK) + a.shape[3:])[:, :L]

    qpos_b = jnp.pad(qpos, (0, pad), mode='edge').reshape(nb, Q_BLOCK)
    qchunk_b = jnp.pad(qchunk, (0, pad), mode='edge').reshape(nb, Q_BLOCK)

    def one_block(blk):
        bqd, bqm, bpos, bchunk = blk
        return attend_block(bqd, bqm, bpos, bchunk, kd, vd, km, krope, vm, kpos, kchunk, rel_bias, lam)

    o_d, o_m = lax.map(one_block, (to_blocks(qd), to_blocks(qm), qpos_b, qchunk_b))
    return from_blocks(o_d), from_blocks(o_m)


def trunk_layer(x, pos, chunk, past, lp, rel_bias, lam_init):
    B, L, _ = x.shape
    h = rmsnorm(x, lp['norm_attn_g'])
    qd, kd, vd, qm, ckv, krope = project(h, pos, lp)
    lamp = lp['diff_lambda'].astype(jnp.float32)
    lam = (jnp.exp(jnp.sum(lamp[0] * lamp[1])) - jnp.exp(jnp.sum(lamp[2] * lamp[3])) + lam_init)
    if past is None:
        km, vm = expand_latent(ckv, lp['mla_w_ukv'])
        o_d, o_m = blocked_attention(qd, qm, pos, chunk, kd, vd, km, krope, vm, pos, chunk, rel_bias, lam)
    else:
        pk, pv, pckv, pkr = past
        P = pk.shape[1]
        kd_all = jnp.concatenate([pk.astype(kd.dtype), kd], axis=1)
        vd_all = jnp.concatenate([pv.astype(vd.dtype), vd], axis=1)
        ckv_all = jnp.concatenate([pckv.astype(ckv.dtype), ckv], axis=1)
        kr_all = jnp.concatenate([pkr.astype(krope.dtype), krope], axis=1)
        past_pos = jnp.arange(P, dtype=jnp.int32)
        kpos = jnp.concatenate([past_pos, pos])
        kchunk = jnp.concatenate([past_pos // CHUNK, chunk])
        km, vm = expand_latent(ckv_all, lp['mla_w_ukv'])
        o_d, o_m = attend_block(qd, qm, pos, chunk, kd_all, vd_all, km, kr_all, vm, kpos, kchunk, rel_bias, lam)
    o_d = rmsnorm(o_d, lp['diff_subln_g']) * (1.0 - lam_init)
    mix = jnp.concatenate([o_d.reshape(B, L, D_DIFF), o_m.reshape(B, L, D_MLA)], axis=-1)
    x = x + mix @ lp['w_out']
    h = rmsnorm(x, lp['norm_ffn_g'])
    x = x + (jax.nn.silu(h @ lp['ffn_w_gate']) * (h @ lp['ffn_w_up'])) @ lp['ffn_w_down']
    return x, (kd, vd, ckv, krope)


def setup_inputs(seed: int = 0) -> dict:
    key = jax.random.key(seed)
    ks = jax.random.split(key, 24)
    f32 = jnp.float32

    def nrm(k, shape, scale):
        return jax.random.normal(k, shape, f32) * scale

    def gain(k, shape):
        return 1.0 + 0.02 * jax.random.normal(k, shape, f32)

    return {
        'x_prompt': nrm(ks[0], (BATCH, SEQ, D_MODEL), 1.0),
        'x_sample': nrm(ks[1], (DEC_BATCH, DEC_SEQ, D_MODEL), 1.0),
        'cache_diff_k': nrm(ks[2], (DEPTH, DEC_BATCH, PAST_LEN, N_DIFF_HEADS, 2, DIFF_HEAD_DIM), 1.0),
        'cache_diff_v': nrm(ks[3], (DEPTH, DEC_BATCH, PAST_LEN, N_DIFF_HEADS, DIFF_V_DIM), 1.0),
        'cache_mla_ckv': nrm(ks[4], (DEPTH, DEC_BATCH, PAST_LEN, MLA_KV_LORA), 1.0),
        'cache_mla_krope': nrm(ks[5], (DEPTH, DEC_BATCH, PAST_LEN, MLA_ROPE_DIM), 1.0),
        'meta_tokens': nrm(ks[6], (N_META, D_MODEL), 1.0),
        'rel_bias': nrm(ks[7], (REL_BUCKETS, N_DIFF_HEADS), 0.5),
        'norm_attn_g': gain(ks[8], (DEPTH, D_MODEL)),
        'w_in': nrm(ks[9], (DEPTH, D_MODEL, D_IN), D_MODEL ** -0.5),
        'diff_lambda': nrm(ks[10], (DEPTH, 4, DIFF_HEAD_DIM), 0.1),
        'diff_subln_g': gain(ks[11], (DEPTH, DIFF_V_DIM)),
        'mla_q_norm_g': gain(ks[12], (DEPTH, MLA_Q_LORA)),
        'mla_w_uq': nrm(ks[13], (DEPTH, MLA_Q_LORA, N_MLA_HEADS * (MLA_NOPE_DIM + MLA_ROPE_DIM)), MLA_Q_LORA ** -0.5),
        'mla_kv_norm_g': gain(ks[14], (DEPTH, MLA_KV_LORA)),
        'mla_w_ukv': nrm(ks[15], (DEPTH, MLA_KV_LORA, N_MLA_HEADS * (MLA_NOPE_DIM + MLA_V_DIM)), MLA_KV_LORA ** -0.5),
        'w_out': nrm(ks[16], (DEPTH, D_MIX, D_MODEL), D_MIX ** -0.5),
        'norm_ffn_g': gain(ks[17], (DEPTH, D_MODEL)),
        'ffn_w_gate': nrm(ks[18], (DEPTH, D_MODEL, D_FF), D_MODEL ** -0.5),
        'ffn_w_up': nrm(ks[19], (DEPTH, D_MODEL, D_FF), D_MODEL ** -0.5),
        'ffn_w_down': nrm(ks[20], (DEPTH, D_FF, D_MODEL), D_FF ** -0.5),
        'final_norm_g': gain(ks[21], (D_MODEL,)),
    }


def reference(x_prompt, x_sample, cache_diff_k, cache_diff_v, cache_mla_ckv, cache_mla_krope,
              meta_tokens, rel_bias, norm_attn_g, w_in, diff_lambda, diff_subln_g,
              mla_q_norm_g, mla_w_uq, mla_kv_norm_g, mla_w_ukv, w_out, norm_ffn_g,
              ffn_w_gate, ffn_w_up, ffn_w_down, final_norm_g):
    B, S, _ = x_prompt.shape
    L = N_META + S
    meta = jnp.broadcast_to(meta_tokens[None].astype(x_prompt.dtype), (B, N_META, D_MODEL))
    xp = jnp.concatenate([meta, x_prompt], axis=1)
    pos_p = jnp.arange(L, dtype=jnp.int32)
    chunk_p = jnp.where(pos_p < N_META, -1, (pos_p - N_META) // CHUNK)
    P = cache_diff_k.shape[2]
    Ds = x_sample.shape[1]
    pos_s = P + jnp.arange(Ds, dtype=jnp.int32)
    chunk_s = pos_s // CHUNK
    xs = x_sample
    rows_p = []
    rows_s = []
    for l in range(DEPTH):
        lp = {
            'norm_attn_g': norm_attn_g[l], 'w_in': w_in[l], 'diff_lambda': diff_lambda[l],
            'diff_subln_g': diff_subln_g[l], 'mla_q_norm_g': mla_q_norm_g[l], 'mla_w_uq': mla_w_uq[l],
            'mla_kv_norm_g': mla_kv_norm_g[l], 'mla_w_ukv': mla_w_ukv[l], 'w_out': w_out[l],
            'norm_ffn_g': norm_ffn_g[l], 'ffn_w_gate': ffn_w_gate[l], 'ffn_w_up': ffn_w_up[l],
            'ffn_w_down': ffn_w_down[l],
        }
        lam_init = 0.8 - 0.6 * math.exp(-0.3 * l)
        xp, rp = trunk_layer(xp, pos_p, chunk_p, None, lp, rel_bias, lam_init)
        past = (cache_diff_k[l], cache_diff_v[l], cache_mla_ckv[l], cache_mla_krope[l])
        xs, rs = trunk_layer(xs, pos_s, chunk_s, past, lp, rel_bias, lam_init)
        rows_p.append(rp)
        rows_s.append(rs)
    y_prompt = rmsnorm(xp[:, N_META:], final_norm_g)
    y_sample = rmsnorm(xs, final_norm_g)
    new_diff_k_prompt = jnp.stack([r[0] for r in rows_p])
    new_diff_v_prompt = jnp.stack([r[1] for r in rows_p])
    new_mla_ckv_prompt = jnp.stack([r[2] for r in rows_p])
    new_mla_krope_prompt = jnp.stack([r[3] for r in rows_p])
    new_diff_k_sample = jnp.stack([r[0] for r in rows_s])
    new_diff_v_sample = jnp.stack([r[1] for r in rows_s])
    new_mla_ckv_sample = jnp.stack([r[2] for r in rows_s])
    new_mla_krope_sample = jnp.stack([r[3] for r in rows_s])
    return (y_prompt, y_sample, new_diff_k_prompt, new_diff_v_prompt, new_mla_ckv_prompt,
            new_mla_krope_prompt, new_diff_k_sample, new_diff_v_sample, new_mla_ckv_sample,
            new_mla_krope_sample)
```

```python
import functools
import math

import jax
import jax.numpy as jnp
import numpy as np
from jax import lax
from jax.experimental import pallas as pl
from jax.experimental.pallas import tpu as pltpu

F32 = jnp.float32
BF16 = jnp.bfloat16

D_MODEL = 2048
CHUNK = 64
N_META = 16
EPS = 1e-6
N_HEADS = 8
DIFF_HEAD_DIM = 64
HEAD_W = 128
D_DIFF_QK = N_HEADS * 2 * DIFF_HEAD_DIM
D_DIFF = N_HEADS * HEAD_W
MLA_NOPE = 128
MLA_ROPE = 64
MLA_QK_PAD = 256
MLA_Q_LORA = 512
MLA_KV_LORA = 256
DIFF_SCALE = DIFF_HEAD_DIM ** -0.5
MLA_SCALE = (MLA_NOPE + MLA_ROPE) ** -0.5
ROPE_THETA = 10000.0
REL_BUCKETS = 32
REL_MAX_DIST = 128
OFF_K = D_DIFF_QK
OFF_V = 2 * D_DIFF_QK
OFF_CQ = OFF_V + D_DIFF
OFF_CKV = OFF_CQ + MLA_Q_LORA
OFF_KR = OFF_CKV + MLA_KV_LORA
D_IN = OFF_KR + MLA_ROPE
D_IN_PAD = OFF_KR + HEAD_W
LAM_INIT = 0.8 - 0.6 * math.exp(-0.3 * 0)
MASK_NEG = -1e30

LANES = 128
VMEM_LIMIT_BYTES = 58 * 1024 * 1024


def _compiler_params(semantics):
    return pltpu.CompilerParams(dimension_semantics=semantics,
                                vmem_limit_bytes=VMEM_LIMIT_BYTES)


def _resident_spec(shape):
    nd = len(shape)
    return pl.BlockSpec(shape, lambda *_: (0,) * nd, pipeline_mode=pl.Buffered(1))


def _rms(x, g):
    ms = jnp.mean(x * x, axis=-1, keepdims=True)
    return x * lax.rsqrt(ms + EPS) * g


def _dot(a, b):
    return jnp.dot(a, b, preferred_element_type=F32)


def _dot_nt(a, b):
    return lax.dot_general(a, b, (((1,), (1,)), ((), ())), preferred_element_type=F32)


def _rope128(r, cos_t, sin_t):
    lane = lax.broadcasted_iota(jnp.int32, r.shape, 1)
    first_half = (lane % MLA_ROPE) < (MLA_ROPE // 2)
    swapped = jnp.where(first_half,
                        pltpu.roll(r, LANES - MLA_ROPE // 2, 1),
                        pltpu.roll(r, MLA_ROPE // 2, 1))
    return r * cos_t + swapped * sin_t


def _proj_kernel(x_ref, cos_ref, sin_ref, g_ref, win_ref, gq_ref, wuq_ref, gkv_ref, wukv_ref,
                 qd_ref, kd_ref, vd_ref, ckv_ref, kr_ref, kdb_ref, vdb_ref, qm_ref, kmr_ref, vm_ref):
    h = _rms(x_ref[0], g_ref[...]).astype(BF16)
    lane = lax.broadcasted_iota(jnp.int32, (h.shape[0], HEAD_W), 1)
    low = lane < DIFF_HEAD_DIM
    cw = 4 * HEAD_W
    for j in range(D_DIFF_QK // cw):
        zq = _dot(h, win_ref[:, j * cw:(j + 1) * cw]) * DIFF_SCALE
        zk = _dot(h, win_ref[:, OFF_K + j * cw:OFF_K + (j + 1) * cw])
        zv = _dot(h, win_ref[:, OFF_V + j * cw:OFF_V + (j + 1) * cw])
        kd_ref[0, :, j * cw:(j + 1) * cw] = zk
        vd_ref[0, :, j * cw:(j + 1) * cw] = zv
        for hh in range(cw // HEAD_W):
            head = j * (cw // HEAD_W) + hh
            sl = slice(hh * HEAD_W, (hh + 1) * HEAD_W)
            qd_ref[0, head, 0] = jnp.where(low, zq[:, sl], 0.0).astype(BF16)
            qd_ref[0, head, 1] = jnp.where(low, 0.0, zq[:, sl]).astype(BF16)
            kdb_ref[0, head] = zk[:, sl].astype(BF16)
            vdb_ref[0, head] = zv[:, sl].astype(BF16)

    cos_t = cos_ref[...]
    sin_t = sin_ref[...]
    cq = _rms(_dot(h, win_ref[:, OFF_CQ:OFF_CKV]), gq_ref[...]).astype(BF16)
    qm = _dot(cq, wuq_ref[...])
    zc = _dot(h, win_ref[:, OFF_CKV:D_IN_PAD])
    ckv = _rms(zc[:, :MLA_KV_LORA], gkv_ref[...])
    ckv_ref[0] = ckv
    kr = _rope128(zc[:, MLA_KV_LORA:], cos_t, sin_t)
    kr_ref[0] = kr[:, :MLA_ROPE]
    krb = kr.astype(BF16)
    kv = _dot(ckv.astype(BF16), wukv_ref[...])
    for head in range(N_HEADS):
        qm_ref[0, head, :, :MLA_NOPE] = qm[:, head * HEAD_W:(head + 1) * HEAD_W].astype(BF16)
        qr = qm[:, D_DIFF + head * HEAD_W:D_DIFF + (head + 1) * HEAD_W]
        qm_ref[0, head, :, MLA_NOPE:] = _rope128(qr, cos_t, sin_t).astype(BF16)
        kmr_ref[0, head, :, :MLA_NOPE] = kv[:, head * 2 * HEAD_W:head * 2 * HEAD_W + HEAD_W].astype(BF16)
        kmr_ref[0, head, :, MLA_NOPE:] = krb
        vm_ref[0, head] = kv[:, head * 2 * HEAD_W + HEAD_W:(head + 1) * 2 * HEAD_W].astype(BF16)


def _proj(x, cos_t, sin_t, g, win, gq, wuq, gkv, wukv, tm):
    nb, length, _ = x.shape
    assert length % tm == 0
    grid = (nb, length // tm)
    row = lambda b, i: (b, i, 0)
    headrow = lambda b, i: (b, 0, i, 0)
    out_shape = (
        jax.ShapeDtypeStruct((nb, N_HEADS, 2, length, HEAD_W), BF16),
        jax.ShapeDtypeStruct((nb, length, D_DIFF_QK), F32),
        jax.ShapeDtypeStruct((nb, length, D_DIFF), F32),
        jax.ShapeDtypeStruct((nb, length, MLA_KV_LORA), F32),
        jax.ShapeDtypeStruct((nb, length, MLA_ROPE), F32),
        jax.ShapeDtypeStruct((nb, N_HEADS, length, HEAD_W), BF16),
        jax.ShapeDtypeStruct((nb, N_HEADS, length, HEAD_W), BF16),
        jax.ShapeDtypeStruct((nb, N_HEADS, length, MLA_QK_PAD), BF16),
        jax.ShapeDtypeStruct((nb, N_HEADS, length, MLA_QK_PAD), BF16),
        jax.ShapeDtypeStruct((nb, N_HEADS, length, HEAD_W), BF16),
    )
    out_specs = (
        pl.BlockSpec((1, N_HEADS, 2, tm, HEAD_W), lambda b, i: (b, 0, 0, i, 0)),
        pl.BlockSpec((1, tm, D_DIFF_QK), row),
        pl.BlockSpec((1, tm, D_DIFF), row),
        pl.BlockSpec((1, tm, MLA_KV_LORA), row),
        pl.BlockSpec((1, tm, MLA_ROPE), row),
        pl.BlockSpec((1, N_HEADS, tm, HEAD_W), headrow),
        pl.BlockSpec((1, N_HEADS, tm, HEAD_W), headrow),
        pl.BlockSpec((1, N_HEADS, tm, MLA_QK_PAD), headrow),
        pl.BlockSpec((1, N_HEADS, tm, MLA_QK_PAD), headrow),
        pl.BlockSpec((1, N_HEADS, tm, HEAD_W), headrow),
    )
    in_specs = [
        pl.BlockSpec((1, tm, D_MODEL), row),
        pl.BlockSpec((tm, LANES), lambda b, i: (i, 0)),
        pl.BlockSpec((tm, LANES), lambda b, i: (i, 0)),
        _resident_spec(g.shape), _resident_spec(win.shape), _resident_spec(gq.shape),
        _resident_spec(wuq.shape), _resident_spec(gkv.shape), _resident_spec(wukv.shape),
    ]
    return pl.pallas_call(
        _proj_kernel, grid=grid, in_specs=in_specs, out_specs=out_specs, out_shape=out_shape,
        compiler_params=_compiler_params(("parallel", "parallel")), name="proj",
    )(x, cos_t, sin_t, g, win, gq, wuq, gkv, wukv)


def _prep_past_kernel(pk_ref, pv_ref, pckv_ref, pkr_ref, wukv_ref, kdb_ref, vdb_ref, kmr_ref, vm_ref):
    kv = _dot(pckv_ref[0].astype(BF16), wukv_ref[...])
    krb = pkr_ref[0].astype(BF16)
    for head in range(N_HEADS):
        sl = slice(head * HEAD_W, (head + 1) * HEAD_W)
        kdb_ref[0, head] = pk_ref[0, :, sl].astype(BF16)
        vdb_ref[0, head] = pv_ref[0, :, sl].astype(BF16)
        kmr_ref[0, head, :, :MLA_NOPE] = kv[:, head * 2 * HEAD_W:head * 2 * HEAD_W + HEAD_W].astype(BF16)
        kmr_ref[0, head, :, MLA_NOPE:] = krb
        vm_ref[0, head] = kv[:, head * 2 * HEAD_W + HEAD_W:(head + 1) * 2 * HEAD_W].astype(BF16)


def _prep_past(pk, pv, pckv, pkr_pad, wukv, tm):
    nb, length, _ = pk.shape
    grid = (nb, length // tm)
    row = lambda b, i: (b, i, 0)
    headrow = lambda b, i: (b, 0, i, 0)
    hm = lambda w: jax.ShapeDtypeStruct((nb, N_HEADS, length, w), BF16)
    hs = lambda w: pl.BlockSpec((1, N_HEADS, tm, w), headrow)
    return pl.pallas_call(
        _prep_past_kernel, grid=grid,
        in_specs=[pl.BlockSpec((1, tm, D_DIFF_QK), row), pl.BlockSpec((1, tm, D_DIFF), row),
                  pl.BlockSpec((1, tm, MLA_KV_LORA), row), pl.BlockSpec((1, tm, HEAD_W), row),
                  _resident_spec(wukv.shape)],
        out_specs=(hs(HEAD_W), hs(HEAD_W), hs(MLA_QK_PAD), hs(HEAD_W)),
        out_shape=(hm(HEAD_W), hm(HEAD_W), hm(MLA_QK_PAD), hm(HEAD_W)),
        compiler_params=_compiler_params(("parallel", "parallel")), name="prep_past",
    )(pk, pv, pckv, pkr_pad, wukv)


def _lambda_full(lam_ref):
    lp = lam_ref[...]
    a = jnp.sum(lp[0:1] * lp[1:2], axis=-1, keepdims=True)
    b = jnp.sum(lp[2:3] * lp[3:4], axis=-1, keepdims=True)
    return jnp.exp(a) - jnp.exp(b) + LAM_INIT


def _finish_diff(o, rows, lam, g):
    od = o[:rows] - lam * o[rows:]
    return _rms(od, g) * (1.0 - LAM_INIT)


def _add_tile(s, ncomp, tile):
    rows, cols = tile.shape
    return (s.reshape(ncomp, rows, cols) + tile[None]).reshape(ncomp * rows, cols)


def _flash_kernel(*refs, tile, ncomp, scale, diff):
    if diff:
        (q_ref, k_ref, v_ref, kmeta_ref, vmeta_ref, bmeta0_ref, bmetafar_ref, bdiag_ref,
         bsub_ref, lam_ref, g_ref, o_ref, m_sc, l_sc, acc_sc) = refs
    else:
        (q_ref, k_ref, v_ref, kmeta_ref, vmeta_ref, bmeta0_ref, bmetafar_ref, bdiag_ref,
         o_ref, m_sc, l_sc, acc_sc) = refs
    seq = k_ref.shape[2]
    nq = seq // tile
    rows = ncomp * tile

    def scores(q, k):
        s = _dot_nt(q, k)
        return s * scale if scale != 1.0 else s

    def online_step(load_q, k, v, add_bias):
        s = add_bias(scores(load_q(), k))
        m_prev = m_sc[...]
        m_new = jnp.maximum(m_prev, jnp.max(s, axis=-1, keepdims=True))
        alpha = jnp.exp(m_prev - m_new)
        p = jnp.exp(s - m_new)
        l_sc[...] = alpha * l_sc[...] + jnp.sum(p, axis=-1, keepdims=True)
        acc_sc[...] = alpha * acc_sc[...] + _dot(p.astype(BF16), v)
        m_sc[...] = m_new

    def q_body(qi, carry):
        q0 = pl.multiple_of(qi * tile, tile)

        def load_q():
            return q_ref[0, 0, :, pl.ds(q0, tile), :].reshape(rows, q_ref.shape[-1])

        s = scores(load_q(), kmeta_ref[0, 0])
        first = (qi == 0).astype(F32)
        bm = first * bmeta0_ref[0] + (1.0 - first) * bmetafar_ref[...]
        s = _add_tile(s, ncomp, bm)
        m0 = jnp.max(s, axis=-1, keepdims=True)
        p = jnp.exp(s - m0)
        m_sc[...] = m0
        l_sc[...] = jnp.sum(p, axis=-1, keepdims=True)
        acc_sc[...] = _dot(p.astype(BF16), vmeta_ref[0, 0])

        def kv_tile(kj):
            k0 = pl.multiple_of(kj * tile, tile)
            return k_ref[0, 0, pl.ds(k0, tile), :], v_ref[0, 0, pl.ds(k0, tile), :]

        def far_body(kj, c):
            k, v = kv_tile(kj)
            online_step(load_q, k, v, lambda s: s)
            return c

        if diff:
            lax.fori_loop(0, jnp.maximum(qi - 1, 0), far_body, 0)

            @pl.when(qi >= 1)
            def _():
                k, v = kv_tile(qi - 1)
                online_step(load_q, k, v, lambda s: _add_tile(s, ncomp, bsub_ref[0]))
        else:
            lax.fori_loop(0, qi, far_body, 0)

        k, v = kv_tile(qi)
        online_step(load_q, k, v, lambda s: _add_tile(s, ncomp, bdiag_ref[0]))

        o = acc_sc[...] / l_sc[...]
        if diff:
            o = _finish_diff(o, tile, _lambda_full(lam_ref), g_ref[...])
        o_ref[0, pl.ds(q0, tile), :] = o.astype(o_ref.dtype)
        return carry

    lax.fori_loop(0, nq, q_body, 0)


def _flash(q, k, v, kmeta, vmeta, bmeta0, bmetafar, bdiag, extra, *, tile, scale, diff):
    nb, nh, ncomp, seq, dk = q.shape
    rows = ncomp * tile
    bh = (lambda b, h: (h, 0, 0)) if bdiag.shape[0] > 1 else (lambda b, h: (0, 0, 0))
    in_specs = [
        pl.BlockSpec((1, 1, ncomp, seq, dk), lambda b, h: (b, h, 0, 0, 0)),
        pl.BlockSpec((1, 1, seq, dk), lambda b, h: (b, h, 0, 0)),
        pl.BlockSpec((1, 1, seq, HEAD_W), lambda b, h: (b, h, 0, 0)),
        pl.BlockSpec((1, 1) + kmeta.shape[2:], lambda b, h: (0, h, 0, 0)),
        pl.BlockSpec((1, 1) + vmeta.shape[2:], lambda b, h: (0, h, 0, 0)),
        pl.BlockSpec((1,) + bmeta0.shape[1:], bh),
        pl.BlockSpec(bmetafar.shape, lambda b, h: (0, 0)),
        pl.BlockSpec((1,) + bdiag.shape[1:], bh),
    ]
    args = [q, k, v, kmeta, vmeta, bmeta0, bmetafar, bdiag]
    if diff:
        bsub, lam_p, g = extra
        in_specs += [pl.BlockSpec((1,) + bsub.shape[1:], bh),
                     pl.BlockSpec(lam_p.shape, lambda b, h: (0, 0)),
                     pl.BlockSpec(g.shape, lambda b, h: (0, 0))]
        args += [bsub, lam_p, g]
    return pl.pallas_call(
        functools.partial(_flash_kernel, tile=tile, ncomp=ncomp, scale=scale, diff=diff),
        grid=(nb, nh), in_specs=in_specs,
        out_specs=pl.BlockSpec((1, seq, HEAD_W), lambda b, h: (b, 0, h)),
        out_shape=jax.ShapeDtypeStruct((nb, seq, nh * HEAD_W), BF16),
        scratch_shapes=[pltpu.VMEM((rows, 1), F32), pltpu.VMEM((rows, 1), F32),
                        pltpu.VMEM((rows, HEAD_W), F32)],
        compiler_params=_compiler_params(("parallel", "parallel")),
        name="flash_diff" if diff else "flash_mla",
    )(*args)


def _attn_dec_kernel(*refs, ncomp, scale, diff):
    if diff:
        q_ref, kp_ref, vp_ref, kn_ref, vn_ref, bp_ref, bn_ref, lam_ref, g_ref, o_ref = refs
    else:
        q_ref, kp_ref, vp_ref, kn_ref, vn_ref, o_ref = refs
    nrow = q_ref.shape[3]
    q = q_ref[0, 0].reshape(ncomp * nrow, q_ref.shape[-1])
    sp = _dot_nt(q, kp_ref[0, 0])
    sn = _dot_nt(q, kn_ref[0, 0])
    if scale != 1.0:
        sp = sp * scale
        sn = sn * scale
    if diff:
        sp = _add_tile(sp, ncomp, bp_ref[0])
        sn = _add_tile(sn, ncomp, bn_ref[0])
    m = jnp.maximum(jnp.max(sp, axis=-1, keepdims=True), jnp.max(sn, axis=-1, keepdims=True))
    pp = jnp.exp(sp - m)
    pn = jnp.exp(sn - m)
    l = jnp.sum(pp, axis=-1, keepdims=True) + jnp.sum(pn, axis=-1, keepdims=True)
    o = (_dot(pp.astype(BF16), vp_ref[0, 0]) + _dot(pn.astype(BF16), vn_ref[0, 0])) / l
    if diff:
        o = _finish_diff(o, nrow, _lambda_full(lam_ref), g_ref[...])
    o_ref[0] = o.astype(o_ref.dtype)


def _attn_dec(q, kp, vp, kn, vn, extra, *, scale, diff):
    nb, nh, ncomp, nrow, dk = q.shape
    past = kp.shape[2]
    bhq = lambda b, h: (b, h, 0, 0)
    in_specs = [
        pl.BlockSpec((1, 1, ncomp, nrow, dk), lambda b, h: (b, h, 0, 0, 0)),
        pl.BlockSpec((1, 1, past, dk), bhq), pl.BlockSpec((1, 1, past, HEAD_W), bhq),
        pl.BlockSpec((1, 1, nrow, dk), bhq), pl.BlockSpec((1, 1, nrow, HEAD_W), bhq),
    ]
    args = [q, kp, vp, kn, vn]
    if diff:
        bias_p, bias_n, lam_p, g = extra
        in_specs += [pl.BlockSpec((1,) + bias_p.shape[1:], lambda b, h: (h, 0, 0)),
                     pl.BlockSpec((1,) + bias_n.shape[1:], lambda b, h: (h, 0, 0)),
                     pl.BlockSpec(lam_p.shape, lambda b, h: (0, 0)),
                     pl.BlockSpec(g.shape, lambda b, h: (0, 0))]
        args += [bias_p, bias_n, lam_p, g]
    return pl.pallas_call(
        functools.partial(_attn_dec_kernel, ncomp=ncomp, scale=scale, diff=diff),
        grid=(nb, nh), in_specs=in_specs,
        out_specs=pl.BlockSpec((1, nrow, HEAD_W), lambda b, h: (b, 0, h)),
        out_shape=jax.ShapeDtypeStruct((nb, nrow, nh * HEAD_W), BF16),
        compiler_params=_compiler_params(("parallel", "parallel")),
        name="attn_dec_diff" if diff else "attn_dec_mla",
    )(*args)


def _outffn_kernel(x_ref, md_ref, mm_ref, wout_ref, gffn_ref, wg_ref, wu_ref, wd_ref, gfin_ref,
                   o_ref, h_sc, acc_sc):
    f = pl.program_id(1)

    @pl.when(f == 0)
    def _():
        x1 = (x_ref[...] + _dot(md_ref[...], wout_ref[:D_DIFF, :])
              + _dot(mm_ref[...], wout_ref[D_DIFF:, :]))
        acc_sc[...] = x1
        h_sc[...] = _rms(x1, gffn_ref[...]).astype(BF16)

    h = h_sc[...]
    gate = _dot(h, wg_ref[...])
    up = _dot(h, wu_ref[...])
    act = (gate * jax.nn.sigmoid(gate) * up).astype(BF16)
    acc_sc[...] += _dot(act, wd_ref[...])

    @pl.when(f == pl.num_programs(1) - 1)
    def _():
        o_ref[...] = _rms(acc_sc[...], gfin_ref[...])


def _outffn(x, mix_d, mix_m, wout, gffn, wg, wu, wd, gfin, tm, tf):
    m = x.shape[0]
    d_ff = wg.shape[1]
    assert m % tm == 0 and d_ff % tf == 0
    row = lambda i, f: (i, 0)
    return pl.pallas_call(
        _outffn_kernel, grid=(m // tm, d_ff // tf),
        in_specs=[pl.BlockSpec((tm, D_MODEL), row), pl.BlockSpec((tm, D_DIFF), row),
                  pl.BlockSpec((tm, D_DIFF), row), _resident_spec(wout.shape),
                  _resident_spec(gffn.shape),
                  pl.BlockSpec((D_MODEL, tf), lambda i, f: (0, f)),
                  pl.BlockSpec((D_MODEL, tf), lambda i, f: (0, f)),
                  pl.BlockSpec((tf, D_MODEL), lambda i, f: (f, 0)),
                  _resident_spec(gfin.shape)],
        out_specs=pl.BlockSpec((tm, D_MODEL), row),
        out_shape=jax.ShapeDtypeStruct((m, D_MODEL), F32),
        scratch_shapes=[pltpu.VMEM((tm, D_MODEL), BF16), pltpu.VMEM((tm, D_MODEL), F32)],
        compiler_params=_compiler_params(("parallel", "arbitrary")), name="outffn",
    )(x, mix_d, mix_m, wout, gffn, wg, wu, wd, gfin)


def _t5_bucket(rel):
    nb = REL_BUCKETS // 2
    max_exact = nb // 2
    ret = jnp.where(rel > 0, nb, 0)
    n = jnp.abs(rel)
    large = max_exact + (jnp.log(jnp.maximum(n, 1).astype(F32) / max_exact)
                         / math.log(REL_MAX_DIST / max_exact) * (nb - max_exact)).astype(jnp.int32)
    large = jnp.minimum(large, nb - 1)
    return ret + jnp.where(n < max_exact, n, large)


def _rope_tables(pos):
    half = MLA_ROPE // 2
    inv_freq = ROPE_THETA ** (-jnp.arange(half, dtype=F32) / half)
    ang = pos.astype(F32)[:, None] * inv_freq[None, :]
    cos, sin = jnp.cos(ang), jnp.sin(ang)
    zero = jnp.zeros_like(cos)
    return (jnp.concatenate([cos, cos, zero, zero], axis=1),
            jnp.concatenate([-sin, sin, zero, zero], axis=1))


def _head_bias(rel_bias, rel):
    return jnp.moveaxis(rel_bias[_t5_bucket(rel)], -1, 0).astype(F32)


PROJ_TM = 256
FLASH_TILE = 512
FFN_TM = 512
FFN_TF = 512


def kernel(x_prompt, x_sample, cache_diff_k, cache_diff_v, cache_mla_ckv, cache_mla_krope,
           meta_tokens, rel_bias, norm_attn_g, w_in, diff_lambda, diff_subln_g, mla_q_norm_g,
           mla_w_uq, mla_kv_norm_g, mla_w_ukv, w_out, norm_ffn_g, ffn_w_gate, ffn_w_up,
           ffn_w_down, final_norm_g):
    nb, seq, _ = x_prompt.shape
    ndec, dseq, _ = x_sample.shape
    past = cache_diff_k.shape[2]
    assert cache_diff_k.shape[0] == 1, "single layer"
    tile = FLASH_TILE
    assert seq % tile == 0 and tile % CHUNK == 0 and tile >= REL_MAX_DIST + LANES
    assert past % CHUNK == 0 and dseq <= CHUNK

    win = jnp.pad(w_in[0], ((0, 0), (0, D_IN_PAD - D_IN))).astype(BF16)
    wuq3 = mla_w_uq[0].reshape(MLA_Q_LORA, N_HEADS, MLA_NOPE + MLA_ROPE)
    wuq = jnp.concatenate(
        [wuq3[:, :, :MLA_NOPE].reshape(MLA_Q_LORA, N_HEADS * MLA_NOPE),
         jnp.pad(wuq3[:, :, MLA_NOPE:], ((0, 0), (0, 0), (0, HEAD_W - MLA_ROPE))
                 ).reshape(MLA_Q_LORA, N_HEADS * HEAD_W)], axis=1).astype(BF16)
    wukv = mla_w_ukv[0].astype(BF16)
    wout = w_out[0].astype(BF16)
    wg = ffn_w_gate[0].astype(BF16)
    wu = ffn_w_up[0].astype(BF16)
    wd = ffn_w_down[0].astype(BF16)
    g_attn = norm_attn_g[0][None]
    g_q = mla_q_norm_g[0][None]
    g_kv = mla_kv_norm_g[0][None]
    g_sub = diff_subln_g[0][None]
    g_ffn = norm_ffn_g[0][None]
    g_fin = final_norm_g[None]
    lam_p = diff_lambda[0]

    pos_meta = jnp.arange(N_META, dtype=jnp.int32)
    pos_frames = N_META + jnp.arange(seq, dtype=jnp.int32)
    pos_dec = past + jnp.arange(dseq, dtype=jnp.int32)
    bias_far = rel_bias[_t5_bucket(jnp.int32(-REL_MAX_DIST))].astype(F32)[:, None, None]
    ti = jnp.arange(tile, dtype=jnp.int32)
    visible = (ti[None, :] // CHUNK) <= (ti[:, None] // CHUNK)
    bdiag_d = jnp.where(visible[None], _head_bias(rel_bias, ti[None, :] - ti[:, None]) - bias_far,
                        MASK_NEG)
    bdiag_m = jnp.where(visible, 0.0, MASK_NEG).astype(F32)[None]
    li = jnp.arange(LANES, dtype=jnp.int32)
    bsub = _head_bias(rel_bias, ti[None, :] - ti[:, None] - tile) - bias_far
    meta_valid = (li < N_META)[None, :]
    rel_meta = jnp.minimum(li, N_META - 1)[None, :] - (N_META + ti)[:, None]
    bmeta0_d = jnp.where(meta_valid[None], _head_bias(rel_bias, rel_meta) - bias_far, MASK_NEG)
    bmetafar = jnp.where(meta_valid, 0.0, MASK_NEG).astype(F32)
    bmeta0_m = jnp.broadcast_to(bmetafar, (tile, LANES))[None]
    di = jnp.arange(dseq, dtype=jnp.int32)
    bias_dp = _head_bias(rel_bias, jnp.arange(past, dtype=jnp.int32)[None, :] - pos_dec[:, None])
    bias_dn = _head_bias(rel_bias, di[None, :] - di[:, None])

    def proj(x, pos, tm):
        cos_t, sin_t = _rope_tables(pos)
        return _proj(x, cos_t, sin_t, g_attn, win, g_q, wuq, g_kv, wukv, tm)

    (qd_p, kd_p, vd_p, ckv_p, kr_p, kdb_p, vdb_p, qm_p, kmr_p, vm_p) = proj(x_prompt, pos_frames, PROJ_TM)
    (qd_s, kd_s, vd_s, ckv_s, kr_s, kdb_s, vdb_s, qm_s, kmr_s, vm_s) = proj(x_sample, pos_dec, dseq)
    (_, kd_m, vd_m, ckv_m, kr_m, kdb_m, vdb_m, _, kmr_m, vm_m) = proj(meta_tokens[None], pos_meta, N_META)

    pad_rows = lambda a: jnp.pad(a, ((0, 0), (0, 0), (0, LANES - N_META), (0, 0)))

    mix_d = _flash(qd_p, kdb_p, vdb_p, pad_rows(kdb_m), pad_rows(vdb_m), bmeta0_d, bmetafar, bdiag_d,
                   (bsub, lam_p, g_sub), tile=tile, scale=1.0, diff=True)
    mix_m = _flash(qm_p[:, :, None], kmr_p, vm_p, pad_rows(kmr_m), pad_rows(vm_m), bmeta0_m, bmetafar,
                   bdiag_m, None, tile=tile, scale=MLA_SCALE, diff=False)

    pk = cache_diff_k[0].reshape(ndec, past, D_DIFF_QK)
    pv = cache_diff_v[0].reshape(ndec, past, D_DIFF)
    pkr_pad = jnp.pad(cache_mla_krope[0], ((0, 0), (0, 0), (0, HEAD_W - MLA_ROPE)))
    kdb_c, vdb_c, kmr_c, vm_c = _prep_past(pk, pv, cache_mla_ckv[0], pkr_pad, wukv, min(past, 512))
    mixs_d = _attn_dec(qd_s, kdb_c, vdb_c, kdb_s, vdb_s, (bias_dp, bias_dn, lam_p, g_sub),
                       scale=1.0, diff=True)
    mixs_m = _attn_dec(qm_s[:, :, None], kmr_c, vm_c, kmr_s, vm_s, None, scale=MLA_SCALE, diff=False)

    def outffn(x, md, mm):
        rows = x.shape[0] * x.shape[1]
        y = _outffn(x.reshape(rows, D_MODEL), md.reshape(rows, D_DIFF), mm.reshape(rows, D_DIFF),
                    wout, g_ffn, wg, wu, wd, g_fin, min(FFN_TM, rows), FFN_TF)
        return y.reshape(x.shape)

    y_prompt = outffn(x_prompt, mix_d, mix_m)
    y_sample = outffn(x_sample, mixs_d, mixs_m)

    def with_meta(meta_rows, frame_rows):
        meta_b = jnp.broadcast_to(meta_rows, (nb,) + meta_rows.shape[1:])
        return jnp.concatenate([meta_b, frame_rows], axis=1)[None]

    length = N_META + seq
    return (
        y_prompt, y_sample,
        with_meta(kd_m, kd_p).reshape(1, nb, length, N_HEADS, 2, DIFF_HEAD_DIM),
        with_meta(vd_m, vd_p).reshape(1, nb, length, N_HEADS, HEAD_W),
        with_meta(ckv_m, ckv_p), with_meta(kr_m, kr_p),
        kd_s.reshape(1, ndec, dseq, N_HEADS, 2, DIFF_HEAD_DIM),
        vd_s.reshape(1, ndec, dseq, N_HEADS, HEAD_W),
        ckv_s[None], kr_s[None],
    )
```

```python
import functools
import math

import jax
import jax.numpy as jnp
from jax import lax
from jax.experimental import pallas as pl
from jax.experimental.pallas import tpu as pltpu

F32 = jnp.float32
BF16 = jnp.bfloat16

D_MODEL = 2048
CHUNK = 64
N_META = 16
EPS = 1e-6
N_HEADS = 8
DIFF_HEAD_DIM = 64
HEAD_W = 128
D_DIFF_QK = N_HEADS * 2 * DIFF_HEAD_DIM
D_DIFF = N_HEADS * HEAD_W
MLA_NOPE = 128
MLA_ROPE = 64
MLA_QK_PAD = 256
MLA_Q_LORA = 512
MLA_KV_LORA = 256
DIFF_SCALE = DIFF_HEAD_DIM ** -0.5
MLA_SCALE = (MLA_NOPE + MLA_ROPE) ** -0.5
ROPE_THETA = 10000.0
REL_BUCKETS = 32
REL_MAX_DIST = 128
OFF_K = D_DIFF_QK
OFF_V = 2 * D_DIFF_QK
OFF_CQ = OFF_V + D_DIFF
OFF_CKV = OFF_CQ + MLA_Q_LORA
OFF_KR = OFF_CKV + MLA_KV_LORA
D_IN = OFF_KR + MLA_ROPE
D_IN_PAD = OFF_KR + HEAD_W
LAM_INIT = 0.8 - 0.6 * math.exp(-0.3 * 0)
MASK_NEG = -1e30
LOG2E = math.log2(math.e)

LANES = 128
VMEM_LIMIT_BYTES = 58 * 1024 * 1024

PROJ_TM = 256
FLASH_TILE = 512
KEY_CHUNK = 128
FFN_TM = 512
FFN_TF = 512


def _compiler_params(semantics):
    return pltpu.CompilerParams(dimension_semantics=semantics,
                                vmem_limit_bytes=VMEM_LIMIT_BYTES)


def _resident_spec(shape):
    nd = len(shape)
    return pl.BlockSpec(shape, lambda *_: (0,) * nd, pipeline_mode=pl.Buffered(1))


def _rms(x, g):
    ms = jnp.mean(x * x, axis=-1, keepdims=True)
    return x * lax.rsqrt(ms + EPS) * g


def _dot(a, b):
    return jnp.dot(a, b, preferred_element_type=F32)


def _dot_nt(a, b):
    return lax.dot_general(a, b, (((1,), (1,)), ((), ())), preferred_element_type=F32)


def _rope128(r, cos_t, sin_t):
    lane = lax.broadcasted_iota(jnp.int32, r.shape, 1)
    first_half = (lane % MLA_ROPE) < (MLA_ROPE // 2)
    swapped = jnp.where(first_half,
                        pltpu.roll(r, LANES - MLA_ROPE // 2, 1),
                        pltpu.roll(r, MLA_ROPE // 2, 1))
    return r * cos_t + swapped * sin_t


def _store_head_v(ref, head, z, v_t):
    if v_t:
        ref[0, head, 0] = z.T.astype(BF16)
    else:
        ref[0, head] = z.astype(BF16)


def _proj_kernel(x_ref, cos_ref, sin_ref, g_ref, win_ref, gq_ref, wuq_ref, gkv_ref, wukv_ref,
                 qd_ref, kd_ref, vd_ref, ckv_ref, kr_ref, kdb_ref, vdb_ref, qm_ref, kmr_ref, vm_ref,
                 *, v_t):
    h = _rms(x_ref[0], g_ref[...]).astype(BF16)
    lane = lax.broadcasted_iota(jnp.int32, (h.shape[0], HEAD_W), 1)
    low = lane < DIFF_HEAD_DIM
    cw = 4 * HEAD_W
    for j in range(D_DIFF_QK // cw):
        zq = _dot(h, win_ref[:, j * cw:(j + 1) * cw]) * DIFF_SCALE
        zk = _dot(h, win_ref[:, OFF_K + j * cw:OFF_K + (j + 1) * cw])
        zv = _dot(h, win_ref[:, OFF_V + j * cw:OFF_V + (j + 1) * cw])
        kd_ref[0, :, j * cw:(j + 1) * cw] = zk
        vd_ref[0, :, j * cw:(j + 1) * cw] = zv
        for hh in range(cw // HEAD_W):
            head = j * (cw // HEAD_W) + hh
            sl = slice(hh * HEAD_W, (hh + 1) * HEAD_W)
            qd_ref[0, head, 0] = jnp.where(low, zq[:, sl], 0.0).astype(BF16)
            qd_ref[0, head, 1] = jnp.where(low, 0.0, zq[:, sl]).astype(BF16)
            kdb_ref[0, head] = zk[:, sl].astype(BF16)
            _store_head_v(vdb_ref, head, zv[:, sl], v_t)

    cos_t = cos_ref[...]
    sin_t = sin_ref[...]
    cq = _rms(_dot(h, win_ref[:, OFF_CQ:OFF_CKV]), gq_ref[...]).astype(BF16)
    qm = _dot(cq, wuq_ref[...])
    zc = _dot(h, win_ref[:, OFF_CKV:D_IN_PAD])
    ckv = _rms(zc[:, :MLA_KV_LORA], gkv_ref[...])
    ckv_ref[0] = ckv
    kr = _rope128(zc[:, MLA_KV_LORA:], cos_t, sin_t)
    kr_ref[0] = kr[:, :MLA_ROPE]
    krb = kr.astype(BF16)
    kv = _dot(ckv.astype(BF16), wukv_ref[...])
    for head in range(N_HEADS):
        qm_ref[0, head, :, :MLA_NOPE] = qm[:, head * HEAD_W:(head + 1) * HEAD_W].astype(BF16)
        qr = qm[:, D_DIFF + head * HEAD_W:D_DIFF + (head + 1) * HEAD_W]
        qm_ref[0, head, :, MLA_NOPE:] = _rope128(qr, cos_t, sin_t).astype(BF16)
        kmr_ref[0, head, :, :MLA_NOPE] = kv[:, head * 2 * HEAD_W:head * 2 * HEAD_W + HEAD_W].astype(BF16)
        kmr_ref[0, head, :, MLA_NOPE:] = krb
        _store_head_v(vm_ref, head, kv[:, head * 2 * HEAD_W + HEAD_W:(head + 1) * 2 * HEAD_W], v_t)


def _proj(x, cos_t, sin_t, g, win, gq, wuq, gkv, wukv, tm, v_t):
    nb, length, _ = x.shape
    assert length % tm == 0
    grid = (nb, length // tm)
    row = lambda b, i: (b, i, 0)
    headrow = lambda b, i: (b, 0, i, 0)
    if v_t:
        v_shape = jax.ShapeDtypeStruct((nb, N_HEADS, length // tm, HEAD_W, tm), BF16)
        v_spec = pl.BlockSpec((1, N_HEADS, 1, HEAD_W, tm), lambda b, i: (b, 0, i, 0, 0))
    else:
        v_shape = jax.ShapeDtypeStruct((nb, N_HEADS, length, HEAD_W), BF16)
        v_spec = pl.BlockSpec((1, N_HEADS, tm, HEAD_W), headrow)
    out_shape = (
        jax.ShapeDtypeStruct((nb, N_HEADS, 2, length, HEAD_W), BF16),
        jax.ShapeDtypeStruct((nb, length, D_DIFF_QK), F32),
        jax.ShapeDtypeStruct((nb, length, D_DIFF), F32),
        jax.ShapeDtypeStruct((nb, length, MLA_KV_LORA), F32),
        jax.ShapeDtypeStruct((nb, length, MLA_ROPE), F32),
        jax.ShapeDtypeStruct((nb, N_HEADS, length, HEAD_W), BF16),
        v_shape,
        jax.ShapeDtypeStruct((nb, N_HEADS, length, MLA_QK_PAD), BF16),
        jax.ShapeDtypeStruct((nb, N_HEADS, length, MLA_QK_PAD), BF16),
        v_shape,
    )
    out_specs = (
        pl.BlockSpec((1, N_HEADS, 2, tm, HEAD_W), lambda b, i: (b, 0, 0, i, 0)),
        pl.BlockSpec((1, tm, D_DIFF_QK), row),
        pl.BlockSpec((1, tm, D_DIFF), row),
        pl.BlockSpec((1, tm, MLA_KV_LORA), row),
        pl.BlockSpec((1, tm, MLA_ROPE), row),
        pl.BlockSpec((1, N_HEADS, tm, HEAD_W), headrow),
        v_spec,
        pl.BlockSpec((1, N_HEADS, tm, MLA_QK_PAD), headrow),
        pl.BlockSpec((1, N_HEADS, tm, MLA_QK_PAD), headrow),
        v_spec,
    )
    in_specs = [
        pl.BlockSpec((1, tm, D_MODEL), row),
        pl.BlockSpec((tm, LANES), lambda b, i: (i, 0)),
        pl.BlockSpec((tm, LANES), lambda b, i: (i, 0)),
        _resident_spec(g.shape), _resident_spec(win.shape), _resident_spec(gq.shape),
        _resident_spec(wuq.shape), _resident_spec(gkv.shape), _resident_spec(wukv.shape),
    ]
    return pl.pallas_call(
        functools.partial(_proj_kernel, v_t=v_t), grid=grid, in_specs=in_specs,
        out_specs=out_specs, out_shape=out_shape,
        compiler_params=_compiler_params(("parallel", "parallel")), name="proj",
    )(x, cos_t, sin_t, g, win, gq, wuq, gkv, wukv)


def _prep_past_kernel(pk_ref, pv_ref, pckv_ref, pkr_ref, wukv_ref, kdb_ref, vdb_ref, kmr_ref, vm_ref):
    kv = _dot(pckv_ref[0].astype(BF16), wukv_ref[...])
    krb = pkr_ref[0].astype(BF16)
    for head in range(N_HEADS):
        sl = slice(head * HEAD_W, (head + 1) * HEAD_W)
        kdb_ref[0, head] = pk_ref[0, :, sl].astype(BF16)
        vdb_ref[0, head] = pv_ref[0, :, sl].astype(BF16)
        kmr_ref[0, head, :, :MLA_NOPE] = kv[:, head * 2 * HEAD_W:head * 2 * HEAD_W + HEAD_W].astype(BF16)
        kmr_ref[0, head, :, MLA_NOPE:] = krb
        vm_ref[0, head] = kv[:, head * 2 * HEAD_W + HEAD_W:(head + 1) * 2 * HEAD_W].astype(BF16)


def _prep_past(pk, pv, pckv, pkr_pad, wukv, tm):
    nb, length, _ = pk.shape
    grid = (nb, length // tm)
    row = lambda b, i: (b, i, 0)
    headrow = lambda b, i: (b, 0, i, 0)
    hm = lambda w: jax.ShapeDtypeStruct((nb, N_HEADS, length, w), BF16)
    hs = lambda w: pl.BlockSpec((1, N_HEADS, tm, w), headrow)
    return pl.pallas_call(
        _prep_past_kernel, grid=grid,
        in_specs=[pl.BlockSpec((1, tm, D_DIFF_QK), row), pl.BlockSpec((1, tm, D_DIFF), row),
                  pl.BlockSpec((1, tm, MLA_KV_LORA), row), pl.BlockSpec((1, tm, HEAD_W), row),
                  _resident_spec(wukv.shape)],
        out_specs=(hs(HEAD_W), hs(HEAD_W), hs(MLA_QK_PAD), hs(HEAD_W)),
        out_shape=(hm(HEAD_W), hm(HEAD_W), hm(MLA_QK_PAD), hm(HEAD_W)),
        compiler_params=_compiler_params(("parallel", "parallel")), name="prep_past",
    )(pk, pv, pckv, pkr_pad, wukv)


def _lambda_full(lam_ref):
    lp = lam_ref[...]
    a = jnp.sum(lp[0:1] * lp[1:2], axis=-1, keepdims=True)
    b = jnp.sum(lp[2:3] * lp[3:4], axis=-1, keepdims=True)
    return jnp.exp(a) - jnp.exp(b) + LAM_INIT


def _add_tile(s, ncomp, tile):
    rows, cols = tile.shape
    return (s.reshape(ncomp, rows, cols) + tile[None]).reshape(ncomp * rows, cols)


def _flash_kernel(*refs, tile, ncomp, scale, diff):
    if diff:
        (q_ref, k_ref, vt_ref, kmeta_ref, vtmeta_ref, bmeta0_ref, bmetafar_ref, bdiag_ref,
         bsub_ref, lam_ref, g_ref, o_ref, s_sc, p_sc, m_sc, l_sc, a_sc, acc_sc) = refs
    else:
        (q_ref, k_ref, vt_ref, kmeta_ref, vtmeta_ref, bmeta0_ref, bmetafar_ref, bdiag_ref,
         o_ref, s_sc, p_sc, m_sc, l_sc, a_sc, acc_sc) = refs
    seq = k_ref.shape[2]
    nq = seq // tile
    qcols = ncomp * tile
    ngroups = 2
    gw = qcols // ngroups
    kblock = vt_ref.shape[-1]
    kb_per_tile = tile // kblock

    def tile_step(q0, k, vts, bias_block, first):
        nkeys = k.shape[0]
        for g in range(ngroups):
            comp, off = divmod(g * gw, tile)
            qg = q_ref[0, 0, comp, pl.ds(q0 + off, gw), :]
            s_sc[0:nkeys, g * gw:(g + 1) * gw] = _dot_nt(k, qg)

        def fold8(x, op):
            parts = [x[r:r + 8] for r in range(0, x.shape[0], 8)]
            while len(parts) > 1:
                parts = [op(parts[i], parts[i + 1]) if i + 1 < len(parts) else parts[i]
                         for i in range(0, len(parts), 2)]
            return parts[0]

        for c in range(qcols // LANES):
            cols = slice(c * LANES, (c + 1) * LANES)
            bcol = (c * LANES) % tile
            biased = bias_block is not None

            def logits(r0, r1):
                sb = s_sc[r0:r1, cols]
                if biased:
                    if scale != 1.0:
                        sb = sb * scale
                    sb = sb + bias_block(bcol, r0, r1)
                return sb

            mx8 = None
            for r0 in range(0, nkeys, KEY_CHUNK):
                part = fold8(logits(r0, min(r0 + KEY_CHUNK, nkeys)), jnp.maximum)
                mx8 = part if mx8 is None else jnp.maximum(mx8, part)
            mx = jnp.max(mx8, axis=0, keepdims=True)
            if scale != 1.0 and not biased:
                mx = mx * scale
            if first:
                m_new = mx
            else:
                m_prev = m_sc[0:1, cols]
                m_new = jnp.maximum(m_prev, mx)
            m_sc[0:1, cols] = m_new
            m2 = m_new * LOG2E
            sum8 = None
            for r0 in range(0, nkeys, KEY_CHUNK):
                r1 = min(r0 + KEY_CHUNK, nkeys)
                if biased:
                    p = jnp.exp2(logits(r0, r1) * LOG2E - m2)
                else:
                    p = jnp.exp2(logits(r0, r1) * (scale * LOG2E) - m2)
                p_sc[r0:r1, cols] = p.astype(BF16)
                part = fold8(p, jnp.add)
                sum8 = part if sum8 is None else sum8 + part
            psum = jnp.sum(sum8, axis=0, keepdims=True)
            if first:
                l_sc[0:1, cols] = psum
            else:
                alpha = jnp.exp2((m_prev - m_new) * LOG2E)
                a_sc[0:1, cols] = alpha
                l_sc[0:1, cols] = alpha * l_sc[0:1, cols] + psum

        for g in range(ngroups):
            gc = slice(g * gw, (g + 1) * gw)
            pv = None
            for j, vt in enumerate(vts):
                kb = vt.shape[1]
                part = _dot(vt, p_sc[j * kb:(j + 1) * kb, gc])
                pv = part if pv is None else pv + part
            if first:
                acc_sc[:, gc] = pv
            else:
                acc_sc[:, gc] = a_sc[0:1, gc] * acc_sc[:, gc] + pv

    def q_body(qi, carry):
        q0 = pl.multiple_of(qi * tile, tile)

        near = (qi == 0).astype(F32)

        def meta_bias(col, r0, r1):
            return (near * bmeta0_ref[0, r0:r1, col:col + LANES]
                    + (1.0 - near) * bmetafar_ref[0, r0:r1, col:col + LANES])

        def tile_bias(ref):
            return lambda col, r0, r1: ref[0, r0:r1, col:col + LANES]

        tile_step(q0, kmeta_ref[0, 0], [vtmeta_ref[0, 0]], meta_bias, True)

        def frame_tile(kj):
            k0 = pl.multiple_of(kj * tile, tile)
            vts = [vt_ref[0, 0, kj * kb_per_tile + j] for j in range(kb_per_tile)]
            return k_ref[0, 0, pl.ds(k0, tile), :], vts

        def far_body(kj, c):
            k, vts = frame_tile(kj)
            tile_step(q0, k, vts, None, False)
            return c

        if diff:
            lax.fori_loop(0, jnp.maximum(qi - 1, 0), far_body, 0)

            @pl.when(qi >= 1)
            def _():
                k, vts = frame_tile(qi - 1)
                tile_step(q0, k, vts, tile_bias(bsub_ref), False)
        else:
            lax.fori_loop(0, qi, far_body, 0)

        k, vts = frame_tile(qi)
        tile_step(q0, k, vts, tile_bias(bdiag_ref), False)

        o_t = acc_sc[...] * (1.0 / l_sc[0:1, :])
        if diff:
            od = (o_t[:, :tile] - _lambda_full(lam_ref) * o_t[:, tile:]).T
            o = _rms(od, g_ref[...]) * (1.0 - LAM_INIT)
        else:
            o = o_t.T
        o_ref[0, pl.ds(q0, tile), :] = o.astype(o_ref.dtype)
        return carry

    lax.fori_loop(0, nq, q_body, 0)


def _flash(q, k, vt, kmeta, vtmeta, bmeta0, bmetafar, bdiag, extra, *, tile, scale, diff):
    nb, nh, ncomp, seq, dk = q.shape
    qcols = ncomp * tile
    bh = (lambda b, h: (h, 0, 0)) if bdiag.shape[0] > 1 else (lambda b, h: (0, 0, 0))
    in_specs = [
        pl.BlockSpec((1, 1, ncomp, seq, dk), lambda b, h: (b, h, 0, 0, 0)),
        pl.BlockSpec((1, 1, seq, dk), lambda b, h: (b, h, 0, 0)),
        pl.BlockSpec((1, 1) + vt.shape[2:], lambda b, h: (b, h, 0, 0, 0)),
        pl.BlockSpec((1, 1) + kmeta.shape[2:], lambda b, h: (0, h, 0, 0)),
        pl.BlockSpec((1, 1) + vtmeta.shape[2:], lambda b, h: (0, h, 0, 0)),
        pl.BlockSpec((1,) + bmeta0.shape[1:], bh),
        pl.BlockSpec(bmetafar.shape, lambda b, h: (0, 0, 0)),
        pl.BlockSpec((1,) + bdiag.shape[1:], bh),
    ]
    args = [q, k, vt, kmeta, vtmeta, bmeta0, bmetafar, bdiag]
    if diff:
        bsub, lam_p, g = extra
        in_specs += [pl.BlockSpec((1,) + bsub.shape[1:], bh),
                     pl.BlockSpec(lam_p.shape, lambda b, h: (0, 0)),
                     pl.BlockSpec(g.shape, lambda b, h: (0, 0))]
        args += [bsub, lam_p, g]
    return pl.pallas_call(
        functools.partial(_flash_kernel, tile=tile, ncomp=ncomp, scale=scale, diff=diff),
        grid=(nb, nh), in_specs=in_specs,
        out_specs=pl.BlockSpec((1, seq, HEAD_W), lambda b, h: (b, 0, h)),
        out_shape=jax.ShapeDtypeStruct((nb, seq, nh * HEAD_W), BF16),
        scratch_shapes=[pltpu.VMEM((tile, qcols), F32), pltpu.VMEM((tile, qcols), BF16),
                        pltpu.VMEM((8, qcols), F32), pltpu.VMEM((8, qcols), F32),
                        pltpu.VMEM((8, qcols), F32), pltpu.VMEM((HEAD_W, qcols), F32)],
        compiler_params=_compiler_params(("parallel", "parallel")),
        name="flash_diff" if diff else "flash_mla",
    )(*args)


def _attn_dec_kernel(*refs, ncomp, scale, diff):
    if diff:
        q_ref, kp_ref, vp_ref, kn_ref, vn_ref, bp_ref, bn_ref, lam_ref, g_ref, o_ref = refs
    else:
        q_ref, kp_ref, vp_ref, kn_ref, vn_ref, o_ref = refs
    nrow = q_ref.shape[3]
    q = q_ref[0, 0].reshape(ncomp * nrow, q_ref.shape[-1])
    sp = _dot_nt(q, kp_ref[0, 0])
    sn = _dot_nt(q, kn_ref[0, 0])
    if scale != 1.0:
        sp = sp * scale
        sn = sn * scale
    if diff:
        sp = _add_tile(sp, ncomp, bp_ref[0])
        sn = _add_tile(sn, ncomp, bn_ref[0])
    m = jnp.maximum(jnp.max(sp, axis=-1, keepdims=True), jnp.max(sn, axis=-1, keepdims=True))
    pp = jnp.exp(sp - m)
    pn = jnp.exp(sn - m)
    l = jnp.sum(pp, axis=-1, keepdims=True) + jnp.sum(pn, axis=-1, keepdims=True)
    o = (_dot(pp.astype(BF16), vp_ref[0, 0]) + _dot(pn.astype(BF16), vn_ref[0, 0])) / l
    if diff:
        od = o[:nrow] - _lambda_full(lam_ref) * o[nrow:]
        o = _rms(od, g_ref[...]) * (1.0 - LAM_INIT)
    o_ref[0] = o.astype(o_ref.dtype)


def _attn_dec(q, kp, vp, kn, vn, extra, *, scale, diff):
    nb, nh, ncomp, nrow, dk = q.shape
    past = kp.shape[2]
    bhq = lambda b, h: (b, h, 0, 0)
    in_specs = [
        pl.BlockSpec((1, 1, ncomp, nrow, dk), lambda b, h: (b, h, 0, 0, 0)),
        pl.BlockSpec((1, 1, past, dk), bhq), pl.BlockSpec((1, 1, past, HEAD_W), bhq),
        pl.BlockSpec((1, 1, nrow, dk), bhq), pl.BlockSpec((1, 1, nrow, HEAD_W), bhq),
    ]
    args = [q, kp, vp, kn, vn]
    if diff:
        bias_p, bias_n, lam_p, g = extra
        in_specs += [pl.BlockSpec((1,) + bias_p.shape[1:], lambda b, h: (h, 0, 0)),
                     pl.BlockSpec((1,) + bias_n.shape[1:], lambda b, h: (h, 0, 0)),
                     pl.BlockSpec(lam_p.shape, lambda b, h: (0, 0)),
                     pl.BlockSpec(g.shape, lambda b, h: (0, 0))]
        args += [bias_p, bias_n, lam_p, g]
    return pl.pallas_call(
        functools.partial(_attn_dec_kernel, ncomp=ncomp, scale=scale, diff=diff),
        grid=(nb, nh), in_specs=in_specs,
        out_specs=pl.BlockSpec((1, nrow, HEAD_W), lambda b, h: (b, 0, h)),
        out_shape=jax.ShapeDtypeStruct((nb, nrow, nh * HEAD_W), BF16),
        compiler_params=_compiler_params(("parallel", "parallel")),
        name="attn_dec_diff" if diff else "attn_dec_mla",
    )(*args)


def _outffn_kernel(x_ref, md_ref, mm_ref, wout_ref, gffn_ref, wg_ref, wu_ref, wd_ref, gfin_ref,
                   o_ref, h_sc, acc_sc):
    f = pl.program_id(1)

    @pl.when(f == 0)
    def _():
        x1 = (x_ref[...] + _dot(md_ref[...], wout_ref[:D_DIFF, :])
              + _dot(mm_ref[...], wout_ref[D_DIFF:, :]))
        acc_sc[...] = x1
        h_sc[...] = _rms(x1, gffn_ref[...]).astype(BF16)

    h = h_sc[...]
    gate = _dot(h, wg_ref[...])
    up = _dot(h, wu_ref[...])
    act = (gate * jax.nn.sigmoid(gate) * up).astype(BF16)
    acc_sc[...] += _dot(act, wd_ref[...])

    @pl.when(f == pl.num_programs(1) - 1)
    def _():
        o_ref[...] = _rms(acc_sc[...], gfin_ref[...])


def _outffn(x, mix_d, mix_m, wout, gffn, wg, wu, wd, gfin, tm, tf):
    m = x.shape[0]
    d_ff = wg.shape[1]
    assert m % tm == 0 and d_ff % tf == 0
    row = lambda i, f: (i, 0)
    return pl.pallas_call(
        _outffn_kernel, grid=(m // tm, d_ff // tf),
        in_specs=[pl.BlockSpec((tm, D_MODEL), row), pl.BlockSpec((tm, D_DIFF), row),
                  pl.BlockSpec((tm, D_DIFF), row), _resident_spec(wout.shape),
                  _resident_spec(gffn.shape),
                  pl.BlockSpec((D_MODEL, tf), lambda i, f: (0, f)),
                  pl.BlockSpec((D_MODEL, tf), lambda i, f: (0, f)),
                  pl.BlockSpec((tf, D_MODEL), lambda i, f: (f, 0)),
                  _resident_spec(gfin.shape)],
        out_specs=pl.BlockSpec((tm, D_MODEL), row),
        out_shape=jax.ShapeDtypeStruct((m, D_MODEL), F32),
        scratch_shapes=[pltpu.VMEM((tm, D_MODEL), BF16), pltpu.VMEM((tm, D_MODEL), F32)],
        compiler_params=_compiler_params(("parallel", "arbitrary")), name="outffn",
    )(x, mix_d, mix_m, wout, gffn, wg, wu, wd, gfin)


def _t5_bucket(rel):
    nb = REL_BUCKETS // 2
    max_exact = nb // 2
    ret = jnp.where(rel > 0, nb, 0)
    n = jnp.abs(rel)
    large = max_exact + (jnp.log(jnp.maximum(n, 1).astype(F32) / max_exact)
                         / math.log(REL_MAX_DIST / max_exact) * (nb - max_exact)).astype(jnp.int32)
    large = jnp.minimum(large, nb - 1)
    return ret + jnp.where(n < max_exact, n, large)


def _rope_tables(pos):
    half = MLA_ROPE // 2
    inv_freq = ROPE_THETA ** (-jnp.arange(half, dtype=F32) / half)
    ang = pos.astype(F32)[:, None] * inv_freq[None, :]
    cos, sin = jnp.cos(ang), jnp.sin(ang)
    zero = jnp.zeros_like(cos)
    return (jnp.concatenate([cos, cos, zero, zero], axis=1),
            jnp.concatenate([-sin, sin, zero, zero], axis=1))


def _head_bias(rel_bias, rel):
    bucket = _t5_bucket(rel)[None]
    out = jnp.zeros((rel_bias.shape[1],) + rel.shape, F32)
    for b in range(REL_BUCKETS):
        out = out + jnp.where(bucket == b, rel_bias[b].astype(F32)[:, None, None], 0.0)
    return out


def kernel(x_prompt, x_sample, cache_diff_k, cache_diff_v, cache_mla_ckv, cache_mla_krope,
           meta_tokens, rel_bias, norm_attn_g, w_in, diff_lambda, diff_subln_g, mla_q_norm_g,
           mla_w_uq, mla_kv_norm_g, mla_w_ukv, w_out, norm_ffn_g, ffn_w_gate, ffn_w_up,
           ffn_w_down, final_norm_g):
    nb, seq, _ = x_prompt.shape
    ndec, dseq, _ = x_sample.shape
    past = cache_diff_k.shape[2]
    assert cache_diff_k.shape[0] == 1, "single layer"
    tile = FLASH_TILE
    assert seq % tile == 0 and tile % CHUNK == 0 and tile % PROJ_TM == 0
    assert tile >= REL_MAX_DIST + LANES
    assert past % CHUNK == 0 and dseq <= CHUNK

    win = jnp.pad(w_in[0], ((0, 0), (0, D_IN_PAD - D_IN))).astype(BF16)
    wuq3 = mla_w_uq[0].reshape(MLA_Q_LORA, N_HEADS, MLA_NOPE + MLA_ROPE)
    wuq = jnp.concatenate(
        [wuq3[:, :, :MLA_NOPE].reshape(MLA_Q_LORA, N_HEADS * MLA_NOPE),
         jnp.pad(wuq3[:, :, MLA_NOPE:], ((0, 0), (0, 0), (0, HEAD_W - MLA_ROPE))
                 ).reshape(MLA_Q_LORA, N_HEADS * HEAD_W)], axis=1).astype(BF16)
    wukv = mla_w_ukv[0].astype(BF16)
    wout = w_out[0].astype(BF16)
    wg = ffn_w_gate[0].astype(BF16)
    wu = ffn_w_up[0].astype(BF16)
    wd = ffn_w_down[0].astype(BF16)
    g_attn = norm_attn_g[0][None]
    g_q = mla_q_norm_g[0][None]
    g_kv = mla_kv_norm_g[0][None]
    g_sub = diff_subln_g[0][None]
    g_ffn = norm_ffn_g[0][None]
    g_fin = final_norm_g[None]
    lam_p = diff_lambda[0]

    pos_meta = jnp.arange(N_META, dtype=jnp.int32)
    pos_frames = N_META + jnp.arange(seq, dtype=jnp.int32)
    pos_dec = past + jnp.arange(dseq, dtype=jnp.int32)
    bias_far = rel_bias[_t5_bucket(jnp.int32(-REL_MAX_DIST))].astype(F32)[:, None, None]
    ti = jnp.arange(tile, dtype=jnp.int32)
    rel_kq = ti[:, None] - ti[None, :]
    visible = (ti[:, None] // CHUNK) <= (ti[None, :] // CHUNK)
    bdiag_d = jnp.where(visible[None], _head_bias(rel_bias, rel_kq) - bias_far, MASK_NEG)
    bdiag_m = jnp.where(visible, 0.0, MASK_NEG).astype(F32)[None]
    bsub = _head_bias(rel_bias, rel_kq - tile) - bias_far
    li = jnp.arange(LANES, dtype=jnp.int32)
    meta_valid = (li < N_META)[:, None]
    rel_meta = jnp.minimum(li, N_META - 1)[:, None] - (N_META + ti)[None, :]
    bmeta0_d = jnp.where(meta_valid[None], _head_bias(rel_bias, rel_meta) - bias_far, MASK_NEG)
    bmetafar = jnp.broadcast_to(jnp.where(meta_valid, 0.0, MASK_NEG).astype(F32), (LANES, tile))[None]
    bmeta0_m = bmetafar
    di = jnp.arange(dseq, dtype=jnp.int32)
    bias_dp = _head_bias(rel_bias, jnp.arange(past, dtype=jnp.int32)[None, :] - pos_dec[:, None])
    bias_dn = _head_bias(rel_bias, di[None, :] - di[:, None])

    def proj(x, pos, tm, v_t):
        cos_t, sin_t = _rope_tables(pos)
        return _proj(x, cos_t, sin_t, g_attn, win, g_q, wuq, g_kv, wukv, tm, v_t)

    (qd_p, kd_p, vd_p, ckv_p, kr_p, kdb_p, vdt_p, qm_p, kmr_p, vmt_p) = proj(x_prompt, pos_frames, PROJ_TM, True)
    (qd_s, kd_s, vd_s, ckv_s, kr_s, kdb_s, vdb_s, qm_s, kmr_s, vm_s) = proj(x_sample, pos_dec, dseq, False)
    (_, kd_m, vd_m, ckv_m, kr_m, kdb_m, vdb_m, _, kmr_m, vm_m) = proj(meta_tokens[None], pos_meta, N_META, False)

    pad_rows = lambda a: jnp.pad(a, ((0, 0), (0, 0), (0, LANES - N_META), (0, 0)))
    pad_rows_t = lambda a: jnp.swapaxes(pad_rows(a), 2, 3)

    mix_d = _flash(qd_p, kdb_p, vdt_p, pad_rows(kdb_m), pad_rows_t(vdb_m), bmeta0_d, bmetafar, bdiag_d,
                   (bsub, lam_p, g_sub), tile=tile, scale=1.0, diff=True)
    mix_m = _flash(qm_p[:, :, None], kmr_p, vmt_p, pad_rows(kmr_m), pad_rows_t(vm_m), bmeta0_m, bmetafar,
                   bdiag_m, None, tile=tile, scale=MLA_SCALE, diff=False)

    pk = cache_diff_k[0].reshape(ndec, past, D_DIFF_QK)
    pv = cache_diff_v[0].reshape(ndec, past, D_DIFF)
    pkr_pad = jnp.pad(cache_mla_krope[0], ((0, 0), (0, 0), (0, HEAD_W - MLA_ROPE)))
    kdb_c, vdb_c, kmr_c, vm_c = _prep_past(pk, pv, cache_mla_ckv[0], pkr_pad, wukv, min(past, 512))
    mixs_d = _attn_dec(qd_s, kdb_c, vdb_c, kdb_s, vdb_s, (bias_dp, bias_dn, lam_p, g_sub),
                       scale=1.0, diff=True)
    mixs_m = _attn_dec(qm_s[:, :, None], kmr_c, vm_c, kmr_s, vm_s, None, scale=MLA_SCALE, diff=False)

    def outffn(x, md, mm):
        rows = x.shape[0] * x.shape[1]
        y = _outffn(x.reshape(rows, D_MODEL), md.reshape(rows, D_DIFF), mm.reshape(rows, D_DIFF),
                    wout, g_ffn, wg, wu, wd, g_fin, min(FFN_TM, rows), FFN_TF)
        return y.reshape(x.shape)

    y_prompt = outffn(x_prompt, mix_d, mix_m)
    y_sample = outffn(x_sample, mixs_d, mixs_m)

    def with_meta(meta_rows, frame_rows):
        meta_b = jnp.broadcast_to(meta_rows, (nb,) + meta_rows.shape[1:])
        return jnp.concatenate([meta_b, frame_rows], axis=1)[None]

    length = N_META + seq
    return (
        y_prompt, y_sample,
        with_meta(kd_m, kd_p).reshape(1, nb, length, N_HEADS, 2, DIFF_HEAD_DIM),
        with_meta(vd_m, vd_p).reshape(1, nb, length, N_HEADS, HEAD_W),
        with_meta(ckv_m, ckv_p), with_meta(kr_m, kr_p),
        kd_s.reshape(1, ndec, dseq, N_HEADS, 2, DIFF_HEAD_DIM),
        vd_s.reshape(1, ndec, dseq, N_HEADS, HEAD_W),
        ckv_s[None], kr_s[None],
    )
```

```python
import functools
import math

import jax
import jax.numpy as jnp
from jax import lax
from jax.experimental import pallas as pl
from jax.experimental.pallas import tpu as pltpu

F32 = jnp.float32
BF16 = jnp.bfloat16

D_MODEL = 2048
CHUNK = 64
N_META = 16
EPS = 1e-6
N_HEADS = 8
DIFF_HEAD_DIM = 64
HEAD_W = 128
D_DIFF_QK = N_HEADS * 2 * DIFF_HEAD_DIM
D_DIFF = N_HEADS * HEAD_W
MLA_NOPE = 128
MLA_ROPE = 64
MLA_QK_PAD = 256
MLA_Q_LORA = 512
MLA_KV_LORA = 256
DIFF_SCALE = DIFF_HEAD_DIM ** -0.5
MLA_SCALE = (MLA_NOPE + MLA_ROPE) ** -0.5
ROPE_THETA = 10000.0
REL_BUCKETS = 32
REL_MAX_DIST = 128
OFF_K = D_DIFF_QK
OFF_V = 2 * D_DIFF_QK
OFF_CQ = OFF_V + D_DIFF
OFF_CKV = OFF_CQ + MLA_Q_LORA
OFF_KR = OFF_CKV + MLA_KV_LORA
D_IN = OFF_KR + MLA_ROPE
D_IN_PAD = OFF_KR + HEAD_W
LAM_INIT = 0.8 - 0.6 * math.exp(-0.3 * 0)
MASK_NEG = -1e30
LOG2E = math.log2(math.e)
Q_DIFF_SCALE = DIFF_SCALE * LOG2E
Q_MLA_SCALE = MLA_SCALE * LOG2E
V_ROWS = HEAD_W + 16

LANES = 128
MXU_COLS = 256
VMEM_LIMIT_BYTES = 58 * 1024 * 1024

PROJ_TM = 256
FLASH_TILE = 512
FIRST_SLOT = 2
MLA_HEADS_PER_STEP = 2
FFN_TM = 512
FFN_TF = 512


def _compiler_params(semantics):
    return pltpu.CompilerParams(dimension_semantics=semantics,
                                vmem_limit_bytes=VMEM_LIMIT_BYTES)


def _resident_spec(shape):
    nd = len(shape)
    return pl.BlockSpec(shape, lambda *_: (0,) * nd, pipeline_mode=pl.Buffered(1))


def _rms(x, g):
    ms = jnp.mean(x * x, axis=-1, keepdims=True)
    return x * lax.rsqrt(ms + EPS) * g


def _dot(a, b):
    return jnp.dot(a, b, preferred_element_type=F32)


def _dot_nt(a, b):
    return lax.dot_general(a, b, (((1,), (1,)), ((), ())), preferred_element_type=F32)


def _rope128(r, cos_t, sin_t):
    lane = lax.broadcasted_iota(jnp.int32, r.shape, 1)
    first_half = (lane % MLA_ROPE) < (MLA_ROPE // 2)
    swapped = jnp.where(first_half,
                        pltpu.roll(r, LANES - MLA_ROPE // 2, 1),
                        pltpu.roll(r, MLA_ROPE // 2, 1))
    return r * cos_t + swapped * sin_t


def _ones_rows(width):
    row = lax.broadcasted_iota(jnp.int32, (V_ROWS - HEAD_W, width), 0)
    return jnp.where(row == 0, 1.0, 0.0)


def _store_head_v(ref, head, z, v_t):
    if v_t:
        ref[0, head, 0] = jnp.concatenate([z.T, _ones_rows(z.shape[0])], axis=0).astype(BF16)
    else:
        ref[0, head] = z.astype(BF16)


def _proj_kernel(x_ref, cos_ref, sin_ref, g_ref, win_ref, gq_ref, wuq_ref, gkv_ref, wukv_ref,
                 qd_ref, kd_ref, vd_ref, ckv_ref, kr_ref, kdb_ref, vdb_ref, qm_ref, kmr_ref, vm_ref,
                 *, v_t):
    h = _rms(x_ref[0], g_ref[...]).astype(BF16)
    lane = lax.broadcasted_iota(jnp.int32, (h.shape[0], HEAD_W), 1)
    low = lane < DIFF_HEAD_DIM
    cw = 4 * HEAD_W
    chunks = [(j, hh, j * (cw // HEAD_W) + hh, slice(hh * HEAD_W, (hh + 1) * HEAD_W))
              for j in range(D_DIFF_QK // cw) for hh in range(cw // HEAD_W)]

    def value_columns(j):
        zv = _dot(h, win_ref[:, OFF_V + j * cw:OFF_V + (j + 1) * cw])
        vd_ref[0, :, j * cw:(j + 1) * cw] = zv
        for jj, hh, head, sl in chunks:
            if jj == j:
                _store_head_v(vdb_ref, head, zv[:, sl], v_t)

    cos_t = cos_ref[...]
    sin_t = sin_ref[...]
    zcq = _dot(h, win_ref[:, OFF_CQ:OFF_CKV])
    zc = _dot(h, win_ref[:, OFF_CKV:D_IN_PAD])
    value_columns(0)
    cq = _rms(zcq, gq_ref[...]).astype(BF16)
    ckv = _rms(zc[:, :MLA_KV_LORA], gkv_ref[...])
    ckv_ref[0] = ckv
    kr = _rope128(zc[:, MLA_KV_LORA:], cos_t, sin_t)
    kr_ref[0] = kr[:, :MLA_ROPE]
    krb = kr.astype(BF16)
    qm = _dot(cq, wuq_ref[...])
    kv = _dot(ckv.astype(BF16), wukv_ref[...])
    for head in range(N_HEADS):
        qm_ref[0, head, :, :MLA_NOPE] = (qm[:, head * HEAD_W:(head + 1) * HEAD_W] * Q_MLA_SCALE).astype(BF16)
        qr = qm[:, D_DIFF + head * HEAD_W:D_DIFF + (head + 1) * HEAD_W]
        qm_ref[0, head, :, MLA_NOPE:] = (_rope128(qr, cos_t, sin_t) * Q_MLA_SCALE).astype(BF16)
        kmr_ref[0, head, :, :MLA_NOPE] = kv[:, head * 2 * HEAD_W:head * 2 * HEAD_W + HEAD_W].astype(BF16)
        kmr_ref[0, head, :, MLA_NOPE:] = krb
        _store_head_v(vm_ref, head, kv[:, head * 2 * HEAD_W + HEAD_W:(head + 1) * 2 * HEAD_W], v_t)

    for j in range(1, D_DIFF // cw):
        value_columns(j)
    for j in range(D_DIFF_QK // cw):
        zk = _dot(h, win_ref[:, OFF_K + j * cw:OFF_K + (j + 1) * cw])
        kd_ref[0, :, j * cw:(j + 1) * cw] = zk
        for jj, hh, head, sl in chunks:
            if jj == j:
                kdb_ref[0, head] = zk[:, sl].astype(BF16)
    for j in range(D_DIFF_QK // cw):
        zq = _dot(h, win_ref[:, j * cw:(j + 1) * cw]) * Q_DIFF_SCALE
        for jj, hh, head, sl in chunks:
            if jj == j:
                qd_ref[0, head, 0] = jnp.where(low, zq[:, sl], 0.0).astype(BF16)
                qd_ref[0, head, 1] = jnp.where(low, 0.0, zq[:, sl]).astype(BF16)


def _proj(x, cos_t, sin_t, g, win, gq, wuq, gkv, wukv, tm, v_t):
    nb, length, _ = x.shape
    assert length % tm == 0
    grid = (nb, length // tm)
    row = lambda b, i: (b, i, 0)
    headrow = lambda b, i: (b, 0, i, 0)
    if v_t:
        v_shape = jax.ShapeDtypeStruct((nb, N_HEADS, length // tm, V_ROWS, tm), BF16)
        v_spec = pl.BlockSpec((1, N_HEADS, 1, V_ROWS, tm), lambda b, i: (b, 0, i, 0, 0))
    else:
        v_shape = jax.ShapeDtypeStruct((nb, N_HEADS, length, HEAD_W), BF16)
        v_spec = pl.BlockSpec((1, N_HEADS, tm, HEAD_W), headrow)
    out_shape = (
        jax.ShapeDtypeStruct((nb, N_HEADS, 2, length, HEAD_W), BF16),
        jax.ShapeDtypeStruct((nb, length, D_DIFF_QK), F32),
        jax.ShapeDtypeStruct((nb, length, D_DIFF), F32),
        jax.ShapeDtypeStruct((nb, length, MLA_KV_LORA), F32),
        jax.ShapeDtypeStruct((nb, length, MLA_ROPE), F32),
        jax.ShapeDtypeStruct((nb, N_HEADS, length, HEAD_W), BF16),
        v_shape,
        jax.ShapeDtypeStruct((nb, N_HEADS, length, MLA_QK_PAD), BF16),
        jax.ShapeDtypeStruct((nb, N_HEADS, length, MLA_QK_PAD), BF16),
        v_shape,
    )
    out_specs = (
        pl.BlockSpec((1, N_HEADS, 2, tm, HEAD_W), lambda b, i: (b, 0, 0, i, 0)),
        pl.BlockSpec((1, tm, D_DIFF_QK), row),
        pl.BlockSpec((1, tm, D_DIFF), row),
        pl.BlockSpec((1, tm, MLA_KV_LORA), row),
        pl.BlockSpec((1, tm, MLA_ROPE), row),
        pl.BlockSpec((1, N_HEADS, tm, HEAD_W), headrow),
        v_spec,
        pl.BlockSpec((1, N_HEADS, tm, MLA_QK_PAD), headrow),
        pl.BlockSpec((1, N_HEADS, tm, MLA_QK_PAD), headrow),
        v_spec,
    )
    in_specs = [
        pl.BlockSpec((1, tm, D_MODEL), row),
        pl.BlockSpec((tm, LANES), lambda b, i: (i, 0)),
        pl.BlockSpec((tm, LANES), lambda b, i: (i, 0)),
        _resident_spec(g.shape), _resident_spec(win.shape), _resident_spec(gq.shape),
        _resident_spec(wuq.shape), _resident_spec(gkv.shape), _resident_spec(wukv.shape),
    ]
    return pl.pallas_call(
        functools.partial(_proj_kernel, v_t=v_t), grid=grid, in_specs=in_specs,
        out_specs=out_specs, out_shape=out_shape,
        compiler_params=_compiler_params(("parallel", "parallel")), name="proj",
    )(x, cos_t, sin_t, g, win, gq, wuq, gkv, wukv)


def _lambda_full(lam_ref):
    lp = lam_ref[...]
    a = jnp.sum(lp[0:1] * lp[1:2], axis=-1, keepdims=True)
    b = jnp.sum(lp[2:3] * lp[3:4], axis=-1, keepdims=True)
    return jnp.exp(a) - jnp.exp(b) + LAM_INIT


def _add_tile(s, ncomp, tile):
    rows, cols = tile.shape
    return (s.reshape(ncomp, rows, cols) + tile[None]).reshape(ncomp * rows, cols)


def _flash_kernel(*refs, tile, ncomp, diff):
    if diff:
        (q_ref, k_ref, vt_ref, kmeta_ref, vtmeta_ref, bmeta_ref, bdiag_ref, corner_ref, lam_ref, g_ref,
         o_ref, s_sc, mx_sc, m_sc, acc_sc, fin_sc) = refs
    else:
        (q_ref, k_ref, vt_ref, kmeta_ref, vtmeta_ref, bmeta_ref, bdiag_ref,
         o_ref, s_sc, mx_sc, m_sc, acc_sc, fin_sc) = refs
    shared_kv = len(k_ref.shape) == 4
    seq = k_ref.shape[-2]
    nq = seq // tile
    qcols = ncomp * tile
    ngroups = 2
    gw = qcols // ngroups
    kblock = vt_ref.shape[-1]
    kb_per_tile = tile // kblock
    n_meta = kmeta_ref.shape[-2]

    def of_comp(ref, comp):
        return ref.at[0, 0] if shared_kv else ref.at[0, 0, comp]

    def q_group(q0, g):
        comp, off = divmod(g * gw, tile)
        return q_ref[0, 0, comp, pl.ds(q0 + off, gw), :]

    def group_cols(g):
        return slice(g * gw, (g + 1) * gw)

    def bias_cols(g):
        start = (g * gw) % tile
        return slice(start, start + gw)

    def diagonal_logits_stage(q0, kj, slot):
        k0 = pl.multiple_of(kj * tile, tile)
        nsub = MXU_COLS // LANES
        for cp in range(qcols // MXU_COLS):
            comp, off = divmod(cp * MXU_COLS, tile)
            nk = off + MXU_COLS
            qg = q_ref[0, 0, comp, pl.ds(q0 + off, MXU_COLS), :]
            k = of_comp(k_ref, comp)[pl.ds(k0, nk), :]
            t = _dot_nt(k, qg) + bdiag_ref[0, 0:nk, off:off + MXU_COLS]
            for c in range(nsub):
                s_sc[slot, cp * nsub + c, 0:nk, :] = t[:, c * LANES:(c + 1) * LANES]
            mx_sc[slot, 0:1, cp * MXU_COLS:(cp + 1) * MXU_COLS] = jnp.max(t, axis=0, keepdims=True)

    def logits_stage(q0, kj, near, slot):
        k0 = pl.multiple_of(kj * tile, tile)
        for g in range(ngroups):
            k = of_comp(k_ref, (g * gw) // tile)[pl.ds(k0, tile), :]
            t = _dot_nt(k, q_group(q0, g))
            blocks = [t[:, c * LANES:(c + 1) * LANES] for c in range(gw // LANES)]
            maxima = [jnp.max(b, axis=0, keepdims=True) for b in blocks]
            base = g * (gw // LANES)
            first = 0
            if diff and (g * gw) % tile == 0:
                top = blocks[0][:tile - LANES]
                corner = blocks[0][tile - LANES:] + corner_ref[0, near]
                s_sc[slot, base, 0:tile - LANES, :] = top
                s_sc[slot, base, tile - LANES:tile, :] = corner
                maxima[0] = jnp.maximum(jnp.max(top, axis=0, keepdims=True),
                                        jnp.max(corner, axis=0, keepdims=True))
                first = 1
            for c in range(first, gw // LANES):
                s_sc[slot, base + c, 0:tile, :] = blocks[c]
            mx_sc[slot, 0:1, group_cols(g)] = jnp.concatenate(maxima, axis=1)

    def meta_logits_stage(q0, mkind, slot):
        for g in range(ngroups):
            gc = group_cols(g)
            km = of_comp(kmeta_ref, (g * gw) // tile)[...]
            t = _dot_nt(km, q_group(q0, g)) + bmeta_ref[0, mkind, :, bias_cols(g)]
            for c in range(gw // LANES):
                s_sc[slot, g * (gw // LANES) + c, tile:tile + n_meta, :] = t[:, c * LANES:(c + 1) * LANES]
            mx_sc[slot, 0:1, gc] = jnp.maximum(mx_sc[slot, 0:1, gc], jnp.max(t, axis=0, keepdims=True))

    def probs_stage(kj, slot, diagonal):
        nsub = MXU_COLS // LANES
        for cp in range(qcols // MXU_COLS):
            comp = (cp * MXU_COLS) // tile
            vts = [(of_comp(vt_ref, comp)[kj * kb_per_tile + j], j * kblock) for j in range(kb_per_tile)]
            if diagonal:
                vts.append((of_comp(vtmeta_ref, comp)[...], tile))
            cols = slice(cp * MXU_COLS, (cp + 1) * MXU_COLS)
            last_query = (cp * MXU_COLS) % tile + MXU_COLS - 1
            m_prev = m_sc[0:1, cols]
            m_new = jnp.maximum(m_prev, mx_sc[slot, 0:1, cols])
            m_sc[0:1, cols] = m_new
            pv = None
            for vt, r0 in vts:
                nk = vt.shape[1]
                if diagonal and r0 < tile and r0 // CHUNK > last_query // CHUNK:
                    continue
                pieces = []
                for c in range(nsub):
                    first_query = (cp * MXU_COLS) % tile + c * LANES
                    seen = nk
                    if diagonal and r0 < tile:
                        seen = min(nk, max(0, (first_query + LANES - 1) // CHUNK * CHUNK + CHUNK - r0))
                    piece = jnp.exp2(s_sc[slot, cp * nsub + c, r0:r0 + seen, :]
                                     - m_new[:, c * LANES:(c + 1) * LANES])
                    if seen < nk:
                        piece = jnp.concatenate([piece, jnp.zeros((nk - seen, LANES), F32)], axis=0)
                    pieces.append(piece)
                p = jnp.concatenate(pieces, axis=1).astype(BF16)
                part = _dot(vt, p)
                pv = part if pv is None else pv + part
            acc_sc[:, cols] = jnp.exp2(m_prev - m_new) * acc_sc[:, cols] + pv

    def write_out(q_start):
        o_t = fin_sc[0:HEAD_W, :] * (1.0 / fin_sc[HEAD_W:HEAD_W + 1, :])
        if diff:
            od = (o_t[:, :tile] - _lambda_full(lam_ref) * o_t[:, tile:]).T
            o = _rms(od, g_ref[...]) * (1.0 - LAM_INIT)
        else:
            o = jnp.concatenate([o_t[:, c * tile:(c + 1) * tile].T for c in range(ncomp)], axis=1)
        o_ref[0, pl.ds(q_start, tile), :] = o.astype(o_ref.dtype)

    def q_body(qi, carry):
        q0 = pl.multiple_of(qi * tile, tile)

        def near(t, q_index):
            return jnp.clip(t - q_index + 2, 0, 1)

        def step(t, slot, next_slot, next_is_diagonal):
            if next_is_diagonal:
                diagonal_logits_stage(q0, t + 1, next_slot)
            else:
                logits_stage(q0, t + 1, near(t + 1, qi), next_slot)
            probs_stage(t, slot, False)

        def next_query_tile_logits():
            qn = jnp.minimum(qi + 1, nq - 1)
            logits_stage(pl.multiple_of(qn * tile, tile), 0, near(0, qn), FIRST_SLOT)

        def last_step(slot):
            meta_logits_stage(q0, jnp.minimum(qi, 1), slot)
            if slot != FIRST_SLOT:
                next_query_tile_logits()
            probs_stage(qi, slot, True)
            if slot == FIRST_SLOT:
                next_query_tile_logits()
            fin_sc[...] = acc_sc[...]

        m_sc[0:1, :] = jnp.full((1, qcols), MASK_NEG, F32)
        acc_sc[...] = jnp.zeros(acc_sc.shape, F32)

        @pl.when(qi == 1)
        def _():
            step(0, FIRST_SLOT, 0, True)
            write_out(q0 - tile)

        @pl.when(qi >= 2)
        def _():
            step(0, FIRST_SLOT, 0, False)
            write_out(q0 - tile)

        def pair(u, c):
            step(2 * u + 1, 0, 1, False)
            step(2 * u + 2, 1, 0, False)
            return c

        lax.fori_loop(0, (qi - 2) // 2, pair, 0)

        @pl.when((qi % 2 == 0) & (qi >= 2))
        def _():
            step(qi - 1, 0, 1, True)
            last_step(1)

        @pl.when((qi % 2 == 1) & (qi >= 3))
        def _():
            step(qi - 2, 0, 1, False)
            step(qi - 1, 1, 0, True)
            last_step(0)

        @pl.when(qi == 1)
        def _():
            last_step(0)

        @pl.when(qi == 0)
        def _():
            last_step(FIRST_SLOT)

        return carry

    assert nq >= 2
    diagonal_logits_stage(0, 0, FIRST_SLOT)
    lax.fori_loop(0, nq, q_body, 0)
    write_out((nq - 1) * tile)


def _flash(q, k, vt, kmeta, vtmeta, bmeta, bdiag, extra, *, tile, diff):
    nb, nh, ncomp, seq, dk = q.shape
    qcols = ncomp * tile
    out_w = HEAD_W if diff else ncomp * HEAD_W
    per_head = bdiag.shape[0] > 1

    def bh(nd):
        return lambda h, b: ((h if per_head else 0),) + (0,) * (nd - 1)

    def per_group(a, batched):
        nd = a.ndim
        return pl.BlockSpec((1, 1) + a.shape[2:], lambda h, b: ((b if batched else 0), h) + (0,) * (nd - 2))

    in_specs = [
        per_group(q, True), per_group(k, True), per_group(vt, True),
        per_group(kmeta, False), per_group(vtmeta, False),
        pl.BlockSpec((1,) + bmeta.shape[1:], bh(4)),
        pl.BlockSpec((1,) + bdiag.shape[1:], bh(3)),
    ]
    args = [q, k, vt, kmeta, vtmeta, bmeta, bdiag]
    if diff:
        corner, lam_p, g = extra
        in_specs += [pl.BlockSpec((1,) + corner.shape[1:], bh(4)),
                     pl.BlockSpec(lam_p.shape, lambda h, b: (0, 0)),
                     pl.BlockSpec(g.shape, lambda h, b: (0, 0))]
        args += [corner, lam_p, g]
    return pl.pallas_call(
        functools.partial(_flash_kernel, tile=tile, ncomp=ncomp, diff=diff),
        grid=(nh, nb), in_specs=in_specs,
        out_specs=pl.BlockSpec((1, seq, out_w), lambda h, b: (b, 0, h)),
        out_shape=jax.ShapeDtypeStruct((nb, seq, nh * out_w), BF16),
        scratch_shapes=[pltpu.VMEM((3, qcols // LANES, tile + kmeta.shape[-2], LANES), F32),
                        pltpu.VMEM((3, 8, qcols), F32),
                        pltpu.VMEM((8, qcols), F32),
                        pltpu.VMEM((V_ROWS, qcols), F32),
                        pltpu.VMEM((V_ROWS, qcols), F32)],
        compiler_params=_compiler_params(("parallel", "parallel")),
        name="flash_diff" if diff else "flash_mla",
    )(*args)


def _softmax_pv(q, kp, vp, kn, vn, bias_p, bias_n, ncomp):
    sp = _dot_nt(q, kp)
    sn = _dot_nt(q, kn)
    if bias_p is not None:
        sp = _add_tile(sp, ncomp, bias_p)
        sn = _add_tile(sn, ncomp, bias_n)
    m = jnp.maximum(jnp.max(sp, axis=-1, keepdims=True), jnp.max(sn, axis=-1, keepdims=True))
    pp = jnp.exp2(sp - m)
    pn = jnp.exp2(sn - m)
    l = jnp.sum(pp, axis=-1, keepdims=True) + jnp.sum(pn, axis=-1, keepdims=True)
    return (_dot(pp.astype(BF16), vp) + _dot(pn.astype(BF16), vn)) / l


def _attn_dec_diff_kernel(q_ref, pk_ref, pv_ref, kn_ref, vn_ref, bp_ref, bn_ref, lam_ref, g_ref, o_ref):
    nrow = q_ref.shape[3]
    lam = _lambda_full(lam_ref)
    for head in range(N_HEADS):
        sl = slice(head * HEAD_W, (head + 1) * HEAD_W)
        q = q_ref[0, head].reshape(2 * nrow, HEAD_W)
        o = _softmax_pv(q, pk_ref[0, :, sl].astype(BF16), pv_ref[0, :, head, :].astype(BF16),
                        kn_ref[0, head], vn_ref[0, head], bp_ref[head], bn_ref[head], 2)
        od = o[:nrow] - lam * o[nrow:]
        o_ref[0, :, sl] = (_rms(od, g_ref[...]) * (1.0 - LAM_INIT)).astype(o_ref.dtype)


def _attn_dec_mla_kernel(q_ref, pckv_ref, pkr_ref, wukv_ref, kn_ref, vn_ref, o_ref):
    ckv = pckv_ref[0].astype(BF16)
    krb = pkr_ref[0].astype(BF16)
    for head in range(N_HEADS):
        kv = _dot(ckv, wukv_ref[:, head * 2 * HEAD_W:(head + 1) * 2 * HEAD_W])
        kp = jnp.concatenate([kv[:, :HEAD_W].astype(BF16), krb], axis=1)
        o = _softmax_pv(q_ref[0, head], kp, kv[:, HEAD_W:].astype(BF16),
                        kn_ref[0, head], vn_ref[0, head], None, None, 1)
        o_ref[0, :, head * HEAD_W:(head + 1) * HEAD_W] = o.astype(o_ref.dtype)


def _attn_dec_diff(q, pk, pv, kn, vn, bias_p, bias_n, lam_p, g, nrow):
    _, nh, ncomp, _, dk = q.shape
    nb, past = pk.shape[:2]
    batch = lambda nd: (lambda b: (b,) + (0,) * (nd - 1))
    rows = lambda b: (0, 0, b, 0)
    return pl.pallas_call(
        _attn_dec_diff_kernel, grid=(nb,),
        in_specs=[pl.BlockSpec((1, nh, ncomp, nrow, dk), lambda b: (0, 0, 0, b, 0)),
                  pl.BlockSpec((1, past, D_DIFF_QK), batch(3)),
                  pl.BlockSpec((1, past, nh, HEAD_W), batch(4)),
                  pl.BlockSpec((1, nh, nrow, dk), rows), pl.BlockSpec((1, nh, nrow, HEAD_W), rows),
                  _resident_spec(bias_p.shape), _resident_spec(bias_n.shape),
                  _resident_spec(lam_p.shape), _resident_spec(g.shape)],
        out_specs=pl.BlockSpec((1, nrow, D_DIFF), batch(3)),
        out_shape=jax.ShapeDtypeStruct((nb, nrow, D_DIFF), BF16),
        compiler_params=_compiler_params(("parallel",)), name="attn_dec_diff",
    )(q, pk, pv, kn, vn, bias_p, bias_n, lam_p, g)


def _attn_dec_mla(q, pckv, pkr_pad, wukv, kn, vn, nrow):
    _, nh, _, dk = q.shape
    nb, past = pckv.shape[:2]
    batch = lambda nd: (lambda b: (b,) + (0,) * (nd - 1))
    rows = lambda b: (0, 0, b, 0)
    return pl.pallas_call(
        _attn_dec_mla_kernel, grid=(nb,),
        in_specs=[pl.BlockSpec((1, nh, nrow, dk), rows),
                  pl.BlockSpec((1, past, MLA_KV_LORA), batch(3)), pl.BlockSpec((1, past, HEAD_W), batch(3)),
                  _resident_spec(wukv.shape),
                  pl.BlockSpec((1, nh, nrow, dk), rows), pl.BlockSpec((1, nh, nrow, HEAD_W), rows)],
        out_specs=pl.BlockSpec((1, nrow, D_DIFF), batch(3)),
        out_shape=jax.ShapeDtypeStruct((nb, nrow, D_DIFF), BF16),
        compiler_params=_compiler_params(("parallel",)), name="attn_dec_mla",
    )(q, pckv, pkr_pad, wukv, kn, vn)


def _outffn_kernel(x_ref, md_ref, mm_ref, wout_ref, gffn_ref, wg_ref, wu_ref, wd_ref, gfin_ref,
                   o_ref, h_sc, acc_sc):
    f = pl.program_id(1)

    @pl.when(f == 0)
    def _():
        x1 = (x_ref[...] + _dot(md_ref[...], wout_ref[:D_DIFF, :])
              + _dot(mm_ref[...], wout_ref[D_DIFF:, :]))
        acc_sc[...] = x1
        h_sc[...] = _rms(x1, gffn_ref[...]).astype(BF16)

    h = h_sc[...]
    gate = _dot(h, wg_ref[...])
    up = _dot(h, wu_ref[...])
    act = (gate * jax.nn.sigmoid(gate) * up).astype(BF16)
    acc_sc[...] += _dot(act, wd_ref[...])

    @pl.when(f == pl.num_programs(1) - 1)
    def _():
        o_ref[...] = _rms(acc_sc[...], gfin_ref[...])


def _outffn(x, mix_d, mix_m, wout, gffn, wg, wu, wd, gfin, tm, tf):
    m = x.shape[0]
    d_ff = wg.shape[1]
    assert m % tm == 0 and d_ff % tf == 0
    row = lambda i, f: (i, 0)
    return pl.pallas_call(
        _outffn_kernel, grid=(m // tm, d_ff // tf),
        in_specs=[pl.BlockSpec((tm, D_MODEL), row), pl.BlockSpec((tm, D_DIFF), row),
                  pl.BlockSpec((tm, D_DIFF), row), _resident_spec(wout.shape),
                  _resident_spec(gffn.shape),
                  pl.BlockSpec((D_MODEL, tf), lambda i, f: (0, f)),
                  pl.BlockSpec((D_MODEL, tf), lambda i, f: (0, f)),
                  pl.BlockSpec((tf, D_MODEL), lambda i, f: (f, 0)),
                  _resident_spec(gfin.shape)],
        out_specs=pl.BlockSpec((tm, D_MODEL), row),
        out_shape=jax.ShapeDtypeStruct((m, D_MODEL), F32),
        scratch_shapes=[pltpu.VMEM((tm, D_MODEL), BF16), pltpu.VMEM((tm, D_MODEL), F32)],
        compiler_params=_compiler_params(("parallel", "arbitrary")), name="outffn",
    )(x, mix_d, mix_m, wout, gffn, wg, wu, wd, gfin)


def _t5_bucket(rel):
    nb = REL_BUCKETS // 2
    max_exact = nb // 2
    ret = jnp.where(rel > 0, nb, 0)
    n = jnp.abs(rel)
    large = max_exact + (jnp.log(jnp.maximum(n, 1).astype(F32) / max_exact)
                         / math.log(REL_MAX_DIST / max_exact) * (nb - max_exact)).astype(jnp.int32)
    large = jnp.minimum(large, nb - 1)
    return ret + jnp.where(n < max_exact, n, large)


def _rope_tables(pos):
    half = MLA_ROPE // 2
    inv_freq = ROPE_THETA ** (-jnp.arange(half, dtype=F32) / half)
    ang = pos.astype(F32)[:, None] * inv_freq[None, :]
    cos, sin = jnp.cos(ang), jnp.sin(ang)
    zero = jnp.zeros_like(cos)
    return (jnp.concatenate([cos, cos, zero, zero], axis=1),
            jnp.concatenate([-sin, sin, zero, zero], axis=1))


def _bias_tiles_kernel(rb_ref, far_ref, bkt_ref, bdiag_ref, corner_ref, bmeta_ref, bdp_ref, bdn_ref,
                       *, tile, past, dseq):
    h = pl.program_id(0)
    n = bkt_ref.shape[1]

    def table(row):
        bkt = bkt_ref[row:row + 1, :]
        out = jnp.zeros((1, n), F32)
        for b in range(REL_BUCKETS):
            out = out + jnp.where(bkt == b, rb_ref[b * N_HEADS + h], 0.0)
        return out

    def toeplitz(tbl, rows, shift, cols):
        x = jnp.broadcast_to(tbl, (rows, n))
        return pltpu.roll(x, shift, 1, stride=1, stride_axis=0)[:, :cols]

    far = rb_ref[far_ref[0] * N_HEADS + h]
    tbl = (table(0) - far) * LOG2E
    masked = MASK_NEG * LOG2E

    def flash_shift(offset):
        return (2 * tile + offset + 1) % n

    kk = lax.broadcasted_iota(jnp.int32, (tile, tile), 0)
    qq = lax.broadcasted_iota(jnp.int32, (tile, tile), 1)
    visible = (kk // CHUNK) <= (qq // CHUNK)
    bdiag_ref[0] = jnp.where(visible, toeplitz(tbl, tile, flash_shift(0), tile), masked)
    corner_ref[0, 0] = jnp.zeros((LANES, LANES), F32)
    corner_ref[0, 1] = toeplitz(tbl, LANES, flash_shift(-LANES), LANES)
    bmeta_ref[0, 0] = toeplitz(tbl, N_META, flash_shift(-N_META), tile)
    bmeta_ref[0, 1] = jnp.zeros((N_META, tile), F32)

    tbl_d = table(1) * LOG2E
    off = past + dseq - 1
    bdp_ref[0] = toeplitz(tbl_d, dseq, (n - dseq + 1) % n, past)
    bdn_ref[0] = toeplitz(tbl_d, dseq, (n - off) % n, dseq)


def _bias_tiles(rel_bias, tile, past, dseq):
    n = 4 * tile
    off = past + dseq - 1
    assert n >= off + dseq and n % LANES == 0
    idx = jnp.arange(n, dtype=jnp.int32)
    bkt = jnp.stack([_t5_bucket((n - 1 - idx) - 2 * tile), _t5_bucket(idx - off)])
    far = _t5_bucket(jnp.full((1,), -REL_MAX_DIST, jnp.int32))
    smem = pl.BlockSpec(memory_space=pltpu.SMEM)
    head = lambda nd: (lambda h: (h,) + (0,) * (nd - 1))
    return pl.pallas_call(
        functools.partial(_bias_tiles_kernel, tile=tile, past=past, dseq=dseq),
        grid=(N_HEADS,),
        in_specs=[smem, smem, pl.BlockSpec(bkt.shape, lambda h: (0, 0))],
        out_specs=(pl.BlockSpec((1, tile, tile), head(3)),
                   pl.BlockSpec((1, 2, LANES, LANES), head(4)),
                   pl.BlockSpec((1, 2, N_META, tile), head(4)),
                   pl.BlockSpec((1, dseq, past), head(3)),
                   pl.BlockSpec((1, dseq, dseq), head(3))),
        out_shape=(jax.ShapeDtypeStruct((N_HEADS, tile, tile), F32),
                   jax.ShapeDtypeStruct((N_HEADS, 2, LANES, LANES), F32),
                   jax.ShapeDtypeStruct((N_HEADS, 2, N_META, tile), F32),
                   jax.ShapeDtypeStruct((N_HEADS, dseq, past), F32),
                   jax.ShapeDtypeStruct((N_HEADS, dseq, dseq), F32)),
        compiler_params=_compiler_params(("parallel",)), name="bias_tiles",
    )(rel_bias.astype(F32).reshape(-1), far, bkt)


def kernel(x_prompt, x_sample, cache_diff_k, cache_diff_v, cache_mla_ckv, cache_mla_krope,
           meta_tokens, rel_bias, norm_attn_g, w_in, diff_lambda, diff_subln_g, mla_q_norm_g,
           mla_w_uq, mla_kv_norm_g, mla_w_ukv, w_out, norm_ffn_g, ffn_w_gate, ffn_w_up,
           ffn_w_down, final_norm_g):
    nb, seq, _ = x_prompt.shape
    ndec, dseq, _ = x_sample.shape
    past = cache_diff_k.shape[2]
    assert cache_diff_k.shape[0] == 1, "single layer"
    tile = FLASH_TILE
    assert seq % tile == 0 and tile % CHUNK == 0 and tile % PROJ_TM == 0
    assert tile >= REL_MAX_DIST + LANES
    assert past % CHUNK == 0 and dseq <= CHUNK

    win = jnp.pad(w_in[0], ((0, 0), (0, D_IN_PAD - D_IN))).astype(BF16)
    wuq3 = mla_w_uq[0].reshape(MLA_Q_LORA, N_HEADS, MLA_NOPE + MLA_ROPE)
    wuq = jnp.concatenate(
        [wuq3[:, :, :MLA_NOPE].reshape(MLA_Q_LORA, N_HEADS * MLA_NOPE),
         jnp.pad(wuq3[:, :, MLA_NOPE:], ((0, 0), (0, 0), (0, HEAD_W - MLA_ROPE))
                 ).reshape(MLA_Q_LORA, N_HEADS * HEAD_W)], axis=1).astype(BF16)
    wukv = mla_w_ukv[0].astype(BF16)
    wout = w_out[0].astype(BF16)
    wg = ffn_w_gate[0].astype(BF16)
    wu = ffn_w_up[0].astype(BF16)
    wd = ffn_w_down[0].astype(BF16)
    g_attn = norm_attn_g[0][None]
    g_q = mla_q_norm_g[0][None]
    g_kv = mla_kv_norm_g[0][None]
    g_sub = diff_subln_g[0][None]
    g_ffn = norm_ffn_g[0][None]
    g_fin = final_norm_g[None]
    lam_p = diff_lambda[0]

    pos_meta = jnp.arange(N_META, dtype=jnp.int32)
    pos_frames = N_META + jnp.arange(seq, dtype=jnp.int32)
    pos_dec = past + jnp.arange(dseq, dtype=jnp.int32)
    bdiag_d, corner_d, bmeta_d, bias_dp, bias_dn = _bias_tiles(rel_bias, tile, past, dseq)
    ti = jnp.arange(tile, dtype=jnp.int32)
    visible = (ti[:, None] // CHUNK) <= (ti[None, :] // CHUNK)
    bdiag_m = (jnp.where(visible, 0.0, MASK_NEG).astype(F32) * LOG2E)[None]
    bmeta_m = jnp.zeros((1, 2, N_META, tile), F32)

    def proj(x, pos, tm, v_t):
        cos_t, sin_t = _rope_tables(pos)
        return _proj(x, cos_t, sin_t, g_attn, win, g_q, wuq, g_kv, wukv, tm, v_t)

    (qd_p, kd_p, vd_p, ckv_p, kr_p, kdb_p, vdt_p, qm_p, kmr_p, vmt_p) = proj(x_prompt, pos_frames, PROJ_TM, True)
    dec_rows = ndec * dseq
    (qd_s, kd_s, vd_s, ckv_s, kr_s, kdb_s, vdb_s, qm_s, kmr_s, vm_s) = proj(
        x_sample.reshape(1, dec_rows, D_MODEL), jnp.tile(pos_dec, ndec), min(PROJ_TM, dec_rows), False)
    (_, kd_m, vd_m, ckv_m, kr_m, kdb_m, vdb_m, _, kmr_m, vm_m) = proj(meta_tokens[None], pos_meta, N_META, False)

    ones_rows = jnp.zeros((1, N_HEADS, V_ROWS - HEAD_W, N_META), BF16).at[:, :, 0, :].set(1.0)

    def meta_vt(v):
        return jnp.concatenate([jnp.swapaxes(v, 2, 3), ones_rows], axis=2)

    mix_d = _flash(qd_p, kdb_p, vdt_p, kdb_m, meta_vt(vdb_m), bmeta_d, bdiag_d,
                   (corner_d, lam_p, g_sub), tile=tile, diff=True)
    pair = lambda a: a.reshape(a.shape[0], N_HEADS // MLA_HEADS_PER_STEP, MLA_HEADS_PER_STEP, *a.shape[2:])
    mix_m = _flash(pair(qm_p), pair(kmr_p), pair(vmt_p), pair(kmr_m), pair(meta_vt(vm_m)), bmeta_m,
                   bdiag_m, None, tile=tile, diff=False)

    pk = cache_diff_k[0].reshape(ndec, past, D_DIFF_QK)
    pv = cache_diff_v[0]
    pkr_pad = jnp.pad(cache_mla_krope[0], ((0, 0), (0, 0), (0, HEAD_W - MLA_ROPE)))
    mixs_d = _attn_dec_diff(qd_s, pk, pv, kdb_s, vdb_s, bias_dp, bias_dn, lam_p, g_sub, dseq)
    mixs_m = _attn_dec_mla(qm_s, cache_mla_ckv[0], pkr_pad, wukv, kmr_s, vm_s, dseq)

    def outffn(x, md, mm):
        rows = x.shape[0] * x.shape[1]
        y = _outffn(x.reshape(rows, D_MODEL), md.reshape(rows, D_DIFF), mm.reshape(rows, D_DIFF),
                    wout, g_ffn, wg, wu, wd, g_fin, min(FFN_TM, rows), FFN_TF)
        return y.reshape(x.shape)

    y_prompt = outffn(x_prompt, mix_d, mix_m)
    y_sample = outffn(x_sample, mixs_d, mixs_m)

    def with_meta(meta_rows, frame_rows):
        meta_b = jnp.broadcast_to(meta_rows, (nb,) + meta_rows.shape[1:])
        return jnp.concatenate([meta_b, frame_rows], axis=1)[None]

    length = N_META + seq
    return (
        y_prompt, y_sample,
        with_meta(kd_m, kd_p).reshape(1, nb, length, N_HEADS, 2, DIFF_HEAD_DIM),
        with_meta(vd_m, vd_p).reshape(1, nb, length, N_HEADS, HEAD_W),
        with_meta(ckv_m, ckv_p), with_meta(kr_m, kr_p),
        kd_s.reshape(1, ndec, dseq, N_HEADS, 2, DIFF_HEAD_DIM),
        vd_s.reshape(1, ndec, dseq, N_HEADS, HEAD_W),
        ckv_s.reshape(1, ndec, dseq, MLA_KV_LORA), kr_s.reshape(1, ndec, dseq, MLA_ROPE),
    )
```

```python
import functools
import math

import jax
import jax.numpy as jnp
from jax import lax
from jax.experimental import pallas as pl
from jax.experimental.pallas import tpu as pltpu

F32 = jnp.float32
BF16 = jnp.bfloat16

D_MODEL = 2048
CHUNK = 64
N_META = 16
EPS = 1e-6
N_HEADS = 8
DIFF_HEAD_DIM = 64
HEAD_W = 128
D_DIFF_QK = N_HEADS * 2 * DIFF_HEAD_DIM
D_DIFF = N_HEADS * HEAD_W
MLA_NOPE = 128
MLA_ROPE = 64
MLA_QK_PAD = 256
MLA_Q_LORA = 512
MLA_KV_LORA = 256
DIFF_SCALE = DIFF_HEAD_DIM ** -0.5
MLA_SCALE = (MLA_NOPE + MLA_ROPE) ** -0.5
ROPE_THETA = 10000.0
REL_BUCKETS = 32
REL_MAX_DIST = 128
OFF_K = D_DIFF_QK
OFF_V = 2 * D_DIFF_QK
OFF_CQ = OFF_V + D_DIFF
OFF_CKV = OFF_CQ + MLA_Q_LORA
OFF_KR = OFF_CKV + MLA_KV_LORA
D_IN = OFF_KR + MLA_ROPE
D_IN_PAD = OFF_KR + HEAD_W
LAM_INIT = 0.8 - 0.6 * math.exp(-0.3 * 0)
MASK_NEG = -1e30
LOG2E = math.log2(math.e)
Q_DIFF_SCALE = DIFF_SCALE * LOG2E
Q_MLA_SCALE = MLA_SCALE * LOG2E
V_ROWS = HEAD_W + 16

LANES = 128
MXU_COLS = 256
VMEM_LIMIT_BYTES = 58 * 1024 * 1024

PROJ_TM = 256
FLASH_TILE = 512
FIRST_SLOT = 2
MLA_HEADS_PER_STEP = 2
FFN_TM = 512
FFN_TF = 512


def _compiler_params(semantics):
    return pltpu.CompilerParams(dimension_semantics=semantics,
                                vmem_limit_bytes=VMEM_LIMIT_BYTES)


def _resident_spec(shape):
    nd = len(shape)
    return pl.BlockSpec(shape, lambda *_: (0,) * nd, pipeline_mode=pl.Buffered(1))


def _rms(x, g):
    ms = jnp.mean(x * x, axis=-1, keepdims=True)
    return x * lax.rsqrt(ms + EPS) * g


def _dot(a, b):
    return jnp.dot(a, b, preferred_element_type=F32)


def _dot_nt(a, b):
    return lax.dot_general(a, b, (((1,), (1,)), ((), ())), preferred_element_type=F32)


def _rope128(r, cos_t, sin_t):
    lane = lax.broadcasted_iota(jnp.int32, r.shape, 1)
    first_half = (lane % MLA_ROPE) < (MLA_ROPE // 2)
    swapped = jnp.where(first_half,
                        pltpu.roll(r, LANES - MLA_ROPE // 2, 1),
                        pltpu.roll(r, MLA_ROPE // 2, 1))
    return r * cos_t + swapped * sin_t


def _ones_rows(width):
    row = lax.broadcasted_iota(jnp.int32, (V_ROWS - HEAD_W, width), 0)
    return jnp.where(row == 0, 1.0, 0.0)


def _store_head_v(ref, head, z, v_t):
    if v_t:
        ref[0, head, 0] = jnp.concatenate([z.T, _ones_rows(z.shape[0])], axis=0).astype(BF16)
    else:
        ref[0, head] = z.astype(BF16)


def _proj_kernel(x_ref, cos_ref, sin_ref, g_ref, win_ref, gq_ref, wuq_ref, gkv_ref, wukv_ref,
                 qd_ref, kd_ref, vd_ref, ckv_ref, kr_ref, kdb_ref, vdb_ref, qm_ref, kmr_ref, vm_ref,
                 *, v_t):
    h = _rms(x_ref[0], g_ref[...]).astype(BF16)
    lane = lax.broadcasted_iota(jnp.int32, (h.shape[0], HEAD_W), 1)
    low = lane < DIFF_HEAD_DIM
    cw = 4 * HEAD_W
    chunks = [(j, hh, j * (cw // HEAD_W) + hh, slice(hh * HEAD_W, (hh + 1) * HEAD_W))
              for j in range(D_DIFF_QK // cw) for hh in range(cw // HEAD_W)]

    def value_columns(j):
        zv = _dot(h, win_ref[:, OFF_V + j * cw:OFF_V + (j + 1) * cw])
        vd_ref[0, :, j * cw:(j + 1) * cw] = zv
        for jj, hh, head, sl in chunks:
            if jj == j:
                _store_head_v(vdb_ref, head, zv[:, sl], v_t)

    cos_t = cos_ref[...]
    sin_t = sin_ref[...]
    zcq = _dot(h, win_ref[:, OFF_CQ:OFF_CKV])
    zc = _dot(h, win_ref[:, OFF_CKV:D_IN_PAD])
    value_columns(0)
    cq = _rms(zcq, gq_ref[...]).astype(BF16)
    ckv = _rms(zc[:, :MLA_KV_LORA], gkv_ref[...])
    ckv_ref[0] = ckv
    kr = _rope128(zc[:, MLA_KV_LORA:], cos_t, sin_t)
    kr_ref[0] = kr[:, :MLA_ROPE]
    krb = kr.astype(BF16)
    qm = _dot(cq, wuq_ref[...])
    kv = _dot(ckv.astype(BF16), wukv_ref[...])
    for head in range(N_HEADS):
        qm_ref[0, head, :, :MLA_NOPE] = (qm[:, head * HEAD_W:(head + 1) * HEAD_W] * Q_MLA_SCALE).astype(BF16)
        qr = qm[:, D_DIFF + head * HEAD_W:D_DIFF + (head + 1) * HEAD_W]
        qm_ref[0, head, :, MLA_NOPE:] = (_rope128(qr, cos_t, sin_t) * Q_MLA_SCALE).astype(BF16)
        kmr_ref[0, head, :, :MLA_NOPE] = kv[:, head * 2 * HEAD_W:head * 2 * HEAD_W + HEAD_W].astype(BF16)
        kmr_ref[0, head, :, MLA_NOPE:] = krb
        _store_head_v(vm_ref, head, kv[:, head * 2 * HEAD_W + HEAD_W:(head + 1) * 2 * HEAD_W], v_t)

    for j in range(1, D_DIFF // cw):
        value_columns(j)
    for j in range(D_DIFF_QK // cw):
        zk = _dot(h, win_ref[:, OFF_K + j * cw:OFF_K + (j + 1) * cw])
        kd_ref[0, :, j * cw:(j + 1) * cw] = zk
        for jj, hh, head, sl in chunks:
            if jj == j:
                kdb_ref[0, head] = zk[:, sl].astype(BF16)
    for j in range(D_DIFF_QK // cw):
        zq = _dot(h, win_ref[:, j * cw:(j + 1) * cw]) * Q_DIFF_SCALE
        for jj, hh, head, sl in chunks:
            if jj == j:
                qd_ref[0, head, 0] = jnp.where(low, zq[:, sl], 0.0).astype(BF16)
                qd_ref[0, head, 1] = jnp.where(low, 0.0, zq[:, sl]).astype(BF16)


def _proj(x, cos_t, sin_t, g, win, gq, wuq, gkv, wukv, tm, v_t):
    nb, length, _ = x.shape
    assert length % tm == 0
    grid = (nb, length // tm)
    row = lambda b, i: (b, i, 0)
    headrow = lambda b, i: (b, 0, i, 0)
    if v_t:
        v_shape = jax.ShapeDtypeStruct((nb, N_HEADS, length // tm, V_ROWS, tm), BF16)
        v_spec = pl.BlockSpec((1, N_HEADS, 1, V_ROWS, tm), lambda b, i: (b, 0, i, 0, 0))
    else:
        v_shape = jax.ShapeDtypeStruct((nb, N_HEADS, length, HEAD_W), BF16)
        v_spec = pl.BlockSpec((1, N_HEADS, tm, HEAD_W), headrow)
    out_shape = (
        jax.ShapeDtypeStruct((nb, N_HEADS, 2, length, HEAD_W), BF16),
        jax.ShapeDtypeStruct((nb, length, D_DIFF_QK), F32),
        jax.ShapeDtypeStruct((nb, length, D_DIFF), F32),
        jax.ShapeDtypeStruct((nb, length, MLA_KV_LORA), F32),
        jax.ShapeDtypeStruct((nb, length, MLA_ROPE), F32),
        jax.ShapeDtypeStruct((nb, N_HEADS, length, HEAD_W), BF16),
        v_shape,
        jax.ShapeDtypeStruct((nb, N_HEADS, length, MLA_QK_PAD), BF16),
        jax.ShapeDtypeStruct((nb, N_HEADS, length, MLA_QK_PAD), BF16),
        v_shape,
    )
    out_specs = (
        pl.BlockSpec((1, N_HEADS, 2, tm, HEAD_W), lambda b, i: (b, 0, 0, i, 0)),
        pl.BlockSpec((1, tm, D_DIFF_QK), row),
        pl.BlockSpec((1, tm, D_DIFF), row),
        pl.BlockSpec((1, tm, MLA_KV_LORA), row),
        pl.BlockSpec((1, tm, MLA_ROPE), row),
        pl.BlockSpec((1, N_HEADS, tm, HEAD_W), headrow),
        v_spec,
        pl.BlockSpec((1, N_HEADS, tm, MLA_QK_PAD), headrow),
        pl.BlockSpec((1, N_HEADS, tm, MLA_QK_PAD), headrow),
        v_spec,
    )
    in_specs = [
        pl.BlockSpec((1, tm, D_MODEL), row),
        pl.BlockSpec((tm, LANES), lambda b, i: (i, 0)),
        pl.BlockSpec((tm, LANES), lambda b, i: (i, 0)),
        _resident_spec(g.shape), _resident_spec(win.shape), _resident_spec(gq.shape),
        _resident_spec(wuq.shape), _resident_spec(gkv.shape), _resident_spec(wukv.shape),
    ]
    return pl.pallas_call(
        functools.partial(_proj_kernel, v_t=v_t), grid=grid, in_specs=in_specs,
        out_specs=out_specs, out_shape=out_shape,
        compiler_params=_compiler_params(("parallel", "parallel")), name="proj",
    )(x, cos_t, sin_t, g, win, gq, wuq, gkv, wukv)


def _lambda_full(lam_ref):
    lp = lam_ref[...]
    a = jnp.sum(lp[0:1] * lp[1:2], axis=-1, keepdims=True)
    b = jnp.sum(lp[2:3] * lp[3:4], axis=-1, keepdims=True)
    return jnp.exp(a) - jnp.exp(b) + LAM_INIT


def _add_tile(s, ncomp, tile):
    rows, cols = tile.shape
    return (s.reshape(ncomp, rows, cols) + tile[None]).reshape(ncomp * rows, cols)


def _flash_kernel(*refs, tile, ncomp, diff):
    if diff:
        (q_ref, k_ref, vt_ref, kmeta_ref, vtmeta_ref, bmeta_ref, bdiag_ref, corner_ref, lam_ref, g_ref,
         o_ref, s_sc, mx_sc, m_sc, acc_sc, fin_sc) = refs
    else:
        (q_ref, k_ref, vt_ref, kmeta_ref, vtmeta_ref, bmeta_ref, bdiag_ref,
         o_ref, s_sc, mx_sc, m_sc, acc_sc, fin_sc) = refs
    shared_kv = len(k_ref.shape) == 4
    seq = k_ref.shape[-2]
    nq = seq // tile
    qcols = ncomp * tile
    ngroups = 2
    gw = qcols // ngroups
    kblock = vt_ref.shape[-1]
    kb_per_tile = tile // kblock
    n_meta = kmeta_ref.shape[-2]

    def of_comp(ref, comp):
        return ref.at[0, 0] if shared_kv else ref.at[0, 0, comp]

    def q_group(q0, g):
        comp, off = divmod(g * gw, tile)
        return q_ref[0, 0, comp, pl.ds(q0 + off, gw), :]

    def group_cols(g):
        return slice(g * gw, (g + 1) * gw)

    def bias_cols(g):
        start = (g * gw) % tile
        return slice(start, start + gw)

    def diagonal_logits_stage(q0, kj, slot):
        k0 = pl.multiple_of(kj * tile, tile)
        nsub = MXU_COLS // LANES
        for cp in range(qcols // MXU_COLS):
            comp, off = divmod(cp * MXU_COLS, tile)
            nk = off + MXU_COLS
            qg = q_ref[0, 0, comp, pl.ds(q0 + off, MXU_COLS), :]
            k = of_comp(k_ref, comp)[pl.ds(k0, nk), :]
            t = _dot_nt(k, qg) + bdiag_ref[0, 0:nk, off:off + MXU_COLS]
            for c in range(nsub):
                s_sc[slot, cp * nsub + c, 0:nk, :] = t[:, c * LANES:(c + 1) * LANES]
            mx_sc[slot, 0:1, cp * MXU_COLS:(cp + 1) * MXU_COLS] = jnp.max(t, axis=0, keepdims=True)

    def logits_stage(q0, kj, near, slot):
        k0 = pl.multiple_of(kj * tile, tile)
        for g in range(ngroups):
            k = of_comp(k_ref, (g * gw) // tile)[pl.ds(k0, tile), :]
            t = _dot_nt(k, q_group(q0, g))
            blocks = [t[:, c * LANES:(c + 1) * LANES] for c in range(gw // LANES)]
            maxima = [jnp.max(b, axis=0, keepdims=True) for b in blocks]
            base = g * (gw // LANES)
            first = 0
            if diff and (g * gw) % tile == 0:
                top = blocks[0][:tile - LANES]
                corner = blocks[0][tile - LANES:] + corner_ref[0, near]
                s_sc[slot, base, 0:tile - LANES, :] = top
                s_sc[slot, base, tile - LANES:tile, :] = corner
                maxima[0] = jnp.maximum(jnp.max(top, axis=0, keepdims=True),
                                        jnp.max(corner, axis=0, keepdims=True))
                first = 1
            for c in range(first, gw // LANES):
                s_sc[slot, base + c, 0:tile, :] = blocks[c]
            mx_sc[slot, 0:1, group_cols(g)] = jnp.concatenate(maxima, axis=1)

    def meta_logits_stage(q0, mkind, slot):
        for g in range(ngroups):
            gc = group_cols(g)
            km = of_comp(kmeta_ref, (g * gw) // tile)[...]
            t = _dot_nt(km, q_group(q0, g)) + bmeta_ref[0, mkind, :, bias_cols(g)]
            for c in range(gw // LANES):
                s_sc[slot, g * (gw // LANES) + c, tile:tile + n_meta, :] = t[:, c * LANES:(c + 1) * LANES]
            mx_sc[slot, 0:1, gc] = jnp.maximum(mx_sc[slot, 0:1, gc], jnp.max(t, axis=0, keepdims=True))

    def probs_stage(kj, slot, diagonal):
        nsub = MXU_COLS // LANES
        for cp in range(qcols // MXU_COLS):
            comp = (cp * MXU_COLS) // tile
            vts = [(of_comp(vt_ref, comp)[kj * kb_per_tile + j], j * kblock) for j in range(kb_per_tile)]
            if diagonal:
                vts.append((of_comp(vtmeta_ref, comp)[...], tile))
            cols = slice(cp * MXU_COLS, (cp + 1) * MXU_COLS)
            last_query = (cp * MXU_COLS) % tile + MXU_COLS - 1
            m_prev = m_sc[0:1, cols]
            m_new = jnp.maximum(m_prev, mx_sc[slot, 0:1, cols])
            m_sc[0:1, cols] = m_new
            pv = None
            for vt, r0 in vts:
                nk = vt.shape[1]
                if diagonal and r0 < tile and r0 // CHUNK > last_query // CHUNK:
                    continue
                pieces = []
                for c in range(nsub):
                    first_query = (cp * MXU_COLS) % tile + c * LANES
                    seen = nk
                    if diagonal and r0 < tile:
                        seen = min(nk, max(0, (first_query + LANES - 1) // CHUNK * CHUNK + CHUNK - r0))
                    piece = jnp.exp2(s_sc[slot, cp * nsub + c, r0:r0 + seen, :]
                                     - m_new[:, c * LANES:(c + 1) * LANES])
                    if seen < nk:
                        piece = jnp.concatenate([piece, jnp.zeros((nk - seen, LANES), F32)], axis=0)
                    pieces.append(piece)
                p = jnp.concatenate(pieces, axis=1).astype(BF16)
                part = _dot(vt, p)
                pv = part if pv is None else pv + part
            acc_sc[:, cols] = jnp.exp2(m_prev - m_new) * acc_sc[:, cols] + pv

    def write_out(q_start):
        o_t = fin_sc[0:HEAD_W, :] * (1.0 / fin_sc[HEAD_W:HEAD_W + 1, :])
        if diff:
            od = (o_t[:, :tile] - _lambda_full(lam_ref) * o_t[:, tile:]).T
            o = _rms(od, g_ref[...]) * (1.0 - LAM_INIT)
        else:
            o = jnp.concatenate([o_t[:, c * tile:(c + 1) * tile].T for c in range(ncomp)], axis=1)
        o_ref[0, pl.ds(q_start, tile), :] = o.astype(o_ref.dtype)

    def q_body(qi, carry):
        q0 = pl.multiple_of(qi * tile, tile)

        def near(t, q_index):
            return jnp.clip(t - q_index + 2, 0, 1)

        def step(t, slot, next_slot, next_is_diagonal):
            if next_is_diagonal:
                diagonal_logits_stage(q0, t + 1, next_slot)
            else:
                logits_stage(q0, t + 1, near(t + 1, qi), next_slot)
            probs_stage(t, slot, False)

        def next_query_tile_logits():
            qn = jnp.minimum(qi + 1, nq - 1)
            logits_stage(pl.multiple_of(qn * tile, tile), 0, near(0, qn), FIRST_SLOT)

        def last_step(slot):
            meta_logits_stage(q0, jnp.minimum(qi, 1), slot)
            if slot != FIRST_SLOT:
                next_query_tile_logits()
            probs_stage(qi, slot, True)
            if slot == FIRST_SLOT:
                next_query_tile_logits()
            fin_sc[...] = acc_sc[...]

        m_sc[0:1, :] = jnp.full((1, qcols), MASK_NEG, F32)
        acc_sc[...] = jnp.zeros(acc_sc.shape, F32)

        @pl.when(qi == 1)
        def _():
            step(0, FIRST_SLOT, 0, True)
            write_out(q0 - tile)

        @pl.when(qi >= 2)
        def _():
            step(0, FIRST_SLOT, 0, False)
            write_out(q0 - tile)

        def pair(u, c):
            step(2 * u + 1, 0, 1, False)
            step(2 * u + 2, 1, 0, False)
            return c

        lax.fori_loop(0, (qi - 2) // 2, pair, 0)

        @pl.when((qi % 2 == 0) & (qi >= 2))
        def _():
            step(qi - 1, 0, 1, True)
            last_step(1)

        @pl.when((qi % 2 == 1) & (qi >= 3))
        def _():
            step(qi - 2, 0, 1, False)
            step(qi - 1, 1, 0, True)
            last_step(0)

        @pl.when(qi == 1)
        def _():
            last_step(0)

        @pl.when(qi == 0)
        def _():
            last_step(FIRST_SLOT)

        return carry

    assert nq >= 2
    diagonal_logits_stage(0, 0, FIRST_SLOT)
    lax.fori_loop(0, nq, q_body, 0)
    write_out((nq - 1) * tile)


def _flash(q, k, vt, kmeta, vtmeta, bmeta, bdiag, extra, *, tile, diff):
    nb, nh, ncomp, seq, dk = q.shape
    qcols = ncomp * tile
    out_w = HEAD_W if diff else ncomp * HEAD_W
    per_head = bdiag.shape[0] > 1

    def bh(nd):
        return lambda h, b: ((h if per_head else 0),) + (0,) * (nd - 1)

    def per_group(a, batched):
        nd = a.ndim
        return pl.BlockSpec((1, 1) + a.shape[2:], lambda h, b: ((b if batched else 0), h) + (0,) * (nd - 2))

    in_specs = [
        per_group(q, True), per_group(k, True), per_group(vt, True),
        per_group(kmeta, False), per_group(vtmeta, False),
        pl.BlockSpec((1,) + bmeta.shape[1:], bh(4)),
        pl.BlockSpec((1,) + bdiag.shape[1:], bh(3)),
    ]
    args = [q, k, vt, kmeta, vtmeta, bmeta, bdiag]
    if diff:
        corner, lam_p, g = extra
        in_specs += [pl.BlockSpec((1,) + corner.shape[1:], bh(4)),
                     pl.BlockSpec(lam_p.shape, lambda h, b: (0, 0)),
                     pl.BlockSpec(g.shape, lambda h, b: (0, 0))]
        args += [corner, lam_p, g]
    return pl.pallas_call(
        functools.partial(_flash_kernel, tile=tile, ncomp=ncomp, diff=diff),
        grid=(nh, nb), in_specs=in_specs,
        out_specs=pl.BlockSpec((1, seq, out_w), lambda h, b: (b, 0, h)),
        out_shape=jax.ShapeDtypeStruct((nb, seq, nh * out_w), BF16),
        scratch_shapes=[pltpu.VMEM((3, qcols // LANES, tile + kmeta.shape[-2], LANES), F32),
                        pltpu.VMEM((3, 8, qcols), F32),
                        pltpu.VMEM((8, qcols), F32),
                        pltpu.VMEM((V_ROWS, qcols), F32),
                        pltpu.VMEM((V_ROWS, qcols), F32)],
        compiler_params=_compiler_params(("parallel", "parallel")),
        name="flash_diff" if diff else "flash_mla",
    )(*args)


def _softmax_pv(q, kp, vp, kn, vn, bias_p, bias_n, ncomp):
    sp = _dot_nt(q, kp)
    sn = _dot_nt(q, kn)
    if bias_p is not None:
        sp = _add_tile(sp, ncomp, bias_p)
        sn = _add_tile(sn, ncomp, bias_n)
    m = jnp.maximum(jnp.max(sp, axis=-1, keepdims=True), jnp.max(sn, axis=-1, keepdims=True))
    pp = jnp.exp2(sp - m)
    pn = jnp.exp2(sn - m)
    l = jnp.sum(pp, axis=-1, keepdims=True) + jnp.sum(pn, axis=-1, keepdims=True)
    return (_dot(pp.astype(vp.dtype), vp) + _dot(pn.astype(BF16), vn)) / l


def _attn_dec_diff_kernel(q_ref, pk_ref, pv_ref, kn_ref, vn_ref, bp_ref, bn_ref, lam_ref, g_ref, o_ref):
    nrow = q_ref.shape[3]
    lam = _lambda_full(lam_ref)
    for head in range(N_HEADS):
        sl = slice(head * HEAD_W, (head + 1) * HEAD_W)
        q = q_ref[0, head].reshape(2 * nrow, HEAD_W)
        past = pk_ref.shape[1]
        o = _softmax_pv(q, pk_ref[0, :, sl].astype(BF16), pv_ref[0, pl.ds(head, past, stride=N_HEADS), :],
                        kn_ref[0, head], vn_ref[0, head], bp_ref[head], bn_ref[head], 2)
        od = o[:nrow] - lam * o[nrow:]
        o_ref[0, :, sl] = (_rms(od, g_ref[...]) * (1.0 - LAM_INIT)).astype(o_ref.dtype)


def _attn_dec_mla_kernel(q_ref, pckv_ref, pkr_ref, wukv_ref, kn_ref, vn_ref, o_ref):
    ckv = pckv_ref[0].astype(BF16)
    krb = pkr_ref[0].astype(BF16)
    for head in range(N_HEADS):
        kv = _dot(ckv, wukv_ref[:, head * 2 * HEAD_W:(head + 1) * 2 * HEAD_W])
        kp = jnp.concatenate([kv[:, :HEAD_W].astype(BF16), krb], axis=1)
        o = _softmax_pv(q_ref[0, head], kp, kv[:, HEAD_W:].astype(BF16),
                        kn_ref[0, head], vn_ref[0, head], None, None, 1)
        o_ref[0, :, head * HEAD_W:(head + 1) * HEAD_W] = o.astype(o_ref.dtype)


def _attn_dec_diff(q, pk, pv, kn, vn, bias_p, bias_n, lam_p, g, nrow):
    _, nh, ncomp, _, dk = q.shape
    nb, past = pk.shape[:2]
    batch = lambda nd: (lambda b: (b,) + (0,) * (nd - 1))
    rows = lambda b: (0, 0, b, 0)
    return pl.pallas_call(
        _attn_dec_diff_kernel, grid=(nb,),
        in_specs=[pl.BlockSpec((1, nh, ncomp, nrow, dk), lambda b: (0, 0, 0, b, 0)),
                  pl.BlockSpec((1, past, D_DIFF_QK), batch(3)),
                  pl.BlockSpec((1, past * nh, HEAD_W), batch(3)),
                  pl.BlockSpec((1, nh, nrow, dk), rows), pl.BlockSpec((1, nh, nrow, HEAD_W), rows),
                  _resident_spec(bias_p.shape), _resident_spec(bias_n.shape),
                  _resident_spec(lam_p.shape), _resident_spec(g.shape)],
        out_specs=pl.BlockSpec((1, nrow, D_DIFF), batch(3)),
        out_shape=jax.ShapeDtypeStruct((nb, nrow, D_DIFF), BF16),
        compiler_params=_compiler_params(("parallel",)), name="attn_dec_diff",
    )(q, pk, pv, kn, vn, bias_p, bias_n, lam_p, g)


def _attn_dec_mla(q, pckv, pkr_pad, wukv, kn, vn, nrow):
    _, nh, _, dk = q.shape
    nb, past = pckv.shape[:2]
    batch = lambda nd: (lambda b: (b,) + (0,) * (nd - 1))
    rows = lambda b: (0, 0, b, 0)
    return pl.pallas_call(
        _attn_dec_mla_kernel, grid=(nb,),
        in_specs=[pl.BlockSpec((1, nh, nrow, dk), rows),
                  pl.BlockSpec((1, past, MLA_KV_LORA), batch(3)), pl.BlockSpec((1, past, HEAD_W), batch(3)),
                  _resident_spec(wukv.shape),
                  pl.BlockSpec((1, nh, nrow, dk), rows), pl.BlockSpec((1, nh, nrow, HEAD_W), rows)],
        out_specs=pl.BlockSpec((1, nrow, D_DIFF), batch(3)),
        out_shape=jax.ShapeDtypeStruct((nb, nrow, D_DIFF), BF16),
        compiler_params=_compiler_params(("parallel",)), name="attn_dec_mla",
    )(q, pckv, pkr_pad, wukv, kn, vn)


def _outffn_kernel(x_ref, md_ref, mm_ref, wout_ref, gffn_ref, wg_ref, wu_ref, wd_ref, gfin_ref,
                   o_ref, h_sc, acc_sc):
    f = pl.program_id(1)

    @pl.when(f == 0)
    def _():
        x1 = (x_ref[...] + _dot(md_ref[...], wout_ref[:D_DIFF, :])
              + _dot(mm_ref[...], wout_ref[D_DIFF:, :]))
        acc_sc[...] = x1
        h_sc[...] = _rms(x1, gffn_ref[...]).astype(BF16)

    h = h_sc[...]
    gate = _dot(h, wg_ref[...])
    up = _dot(h, wu_ref[...])
    act = (gate * jax.nn.sigmoid(gate) * up).astype(BF16)
    acc_sc[...] += _dot(act, wd_ref[...])

    @pl.when(f == pl.num_programs(1) - 1)
    def _():
        o_ref[...] = _rms(acc_sc[...], gfin_ref[...])


def _outffn(x, mix_d, mix_m, wout, gffn, wg, wu, wd, gfin, tm, tf):
    m = x.shape[0]
    d_ff = wg.shape[1]
    assert m % tm == 0 and d_ff % tf == 0
    row = lambda i, f: (i, 0)
    return pl.pallas_call(
        _outffn_kernel, grid=(m // tm, d_ff // tf),
        in_specs=[pl.BlockSpec((tm, D_MODEL), row), pl.BlockSpec((tm, D_DIFF), row),
                  pl.BlockSpec((tm, D_DIFF), row), _resident_spec(wout.shape),
                  _resident_spec(gffn.shape),
                  pl.BlockSpec((D_MODEL, tf), lambda i, f: (0, f)),
                  pl.BlockSpec((D_MODEL, tf), lambda i, f: (0, f)),
                  pl.BlockSpec((tf, D_MODEL), lambda i, f: (f, 0)),
                  _resident_spec(gfin.shape)],
        out_specs=pl.BlockSpec((tm, D_MODEL), row),
        out_shape=jax.ShapeDtypeStruct((m, D_MODEL), F32),
        scratch_shapes=[pltpu.VMEM((tm, D_MODEL), BF16), pltpu.VMEM((tm, D_MODEL), F32)],
        compiler_params=_compiler_params(("parallel", "arbitrary")), name="outffn",
    )(x, mix_d, mix_m, wout, gffn, wg, wu, wd, gfin)


def _t5_bucket(rel):
    nb = REL_BUCKETS // 2
    max_exact = nb // 2
    ret = jnp.where(rel > 0, nb, 0)
    n = jnp.abs(rel)
    large = max_exact + (jnp.log(jnp.maximum(n, 1).astype(F32) / max_exact)
                         / math.log(REL_MAX_DIST / max_exact) * (nb - max_exact)).astype(jnp.int32)
    large = jnp.minimum(large, nb - 1)
    return ret + jnp.where(n < max_exact, n, large)


def _rope_tables(pos):
    half = MLA_ROPE // 2
    inv_freq = ROPE_THETA ** (-jnp.arange(half, dtype=F32) / half)
    ang = pos.astype(F32)[:, None] * inv_freq[None, :]
    cos, sin = jnp.cos(ang), jnp.sin(ang)
    zero = jnp.zeros_like(cos)
    return (jnp.concatenate([cos, cos, zero, zero], axis=1),
            jnp.concatenate([-sin, sin, zero, zero], axis=1))


def _bias_tiles_kernel(rb_ref, far_ref, bkt_ref, bdiag_ref, corner_ref, bmeta_ref, bdp_ref, bdn_ref,
                       *, tile, past, dseq):
    h = pl.program_id(0)
    n = bkt_ref.shape[1]

    def table(row):
        bkt = bkt_ref[row:row + 1, :]
        out = jnp.zeros((1, n), F32)
        for b in range(REL_BUCKETS):
            out = out + jnp.where(bkt == b, rb_ref[b * N_HEADS + h], 0.0)
        return out

    def toeplitz(tbl, rows, shift, cols):
        x = jnp.broadcast_to(tbl, (rows, n))
        return pltpu.roll(x, shift, 1, stride=1, stride_axis=0)[:, :cols]

    far = rb_ref[far_ref[0] * N_HEADS + h]
    tbl = (table(0) - far) * LOG2E
    masked = MASK_NEG * LOG2E

    def flash_shift(offset):
        return (2 * tile + offset + 1) % n

    kk = lax.broadcasted_iota(jnp.int32, (tile, tile), 0)
    qq = lax.broadcasted_iota(jnp.int32, (tile, tile), 1)
    visible = (kk // CHUNK) <= (qq // CHUNK)
    bdiag_ref[0] = jnp.where(visible, toeplitz(tbl, tile, flash_shift(0), tile), masked)
    corner_ref[0, 0] = jnp.zeros((LANES, LANES), F32)
    corner_ref[0, 1] = toeplitz(tbl, LANES, flash_shift(-LANES), LANES)
    bmeta_ref[0, 0] = toeplitz(tbl, N_META, flash_shift(-N_META), tile)
    bmeta_ref[0, 1] = jnp.zeros((N_META, tile), F32)

    tbl_d = table(1) * LOG2E
    off = past + dseq - 1
    bdp_ref[0] = toeplitz(tbl_d, dseq, (n - dseq + 1) % n, past)
    bdn_ref[0] = toeplitz(tbl_d, dseq, (n - off) % n, dseq)


def _bias_tiles(rel_bias, tile, past, dseq):
    n = 4 * tile
    off = past + dseq - 1
    assert n >= off + dseq and n % LANES == 0
    idx = jnp.arange(n, dtype=jnp.int32)
    bkt = jnp.stack([_t5_bucket((n - 1 - idx) - 2 * tile), _t5_bucket(idx - off)])
    far = _t5_bucket(jnp.full((1,), -REL_MAX_DIST, jnp.int32))
    smem = pl.BlockSpec(memory_space=pltpu.SMEM)
    head = lambda nd: (lambda h: (h,) + (0,) * (nd - 1))
    return pl.pallas_call(
        functools.partial(_bias_tiles_kernel, tile=tile, past=past, dseq=dseq),
        grid=(N_HEADS,),
        in_specs=[smem, smem, pl.BlockSpec(bkt.shape, lambda h: (0, 0))],
        out_specs=(pl.BlockSpec((1, tile, tile), head(3)),
                   pl.BlockSpec((1, 2, LANES, LANES), head(4)),
                   pl.BlockSpec((1, 2, N_META, tile), head(4)),
                   pl.BlockSpec((1, dseq, past), head(3)),
                   pl.BlockSpec((1, dseq, dseq), head(3))),
        out_shape=(jax.ShapeDtypeStruct((N_HEADS, tile, tile), F32),
                   jax.ShapeDtypeStruct((N_HEADS, 2, LANES, LANES), F32),
                   jax.ShapeDtypeStruct((N_HEADS, 2, N_META, tile), F32),
                   jax.ShapeDtypeStruct((N_HEADS, dseq, past), F32),
                   jax.ShapeDtypeStruct((N_HEADS, dseq, dseq), F32)),
        compiler_params=_compiler_params(("parallel",)), name="bias_tiles",
    )(rel_bias.astype(F32).reshape(-1), far, bkt)


def kernel(x_prompt, x_sample, cache_diff_k, cache_diff_v, cache_mla_ckv, cache_mla_krope,
           meta_tokens, rel_bias, norm_attn_g, w_in, diff_lambda, diff_subln_g, mla_q_norm_g,
           mla_w_uq, mla_kv_norm_g, mla_w_ukv, w_out, norm_ffn_g, ffn_w_gate, ffn_w_up,
           ffn_w_down, final_norm_g):
    nb, seq, _ = x_prompt.shape
    ndec, dseq, _ = x_sample.shape
    past = cache_diff_k.shape[2]
    assert cache_diff_k.shape[0] == 1, "single layer"
    tile = FLASH_TILE
    assert seq % tile == 0 and tile % CHUNK == 0 and tile % PROJ_TM == 0
    assert tile >= REL_MAX_DIST + LANES
    assert past % CHUNK == 0 and dseq <= CHUNK

    win = jnp.pad(w_in[0], ((0, 0), (0, D_IN_PAD - D_IN))).astype(BF16)
    wuq3 = mla_w_uq[0].reshape(MLA_Q_LORA, N_HEADS, MLA_NOPE + MLA_ROPE)
    wuq = jnp.concatenate(
        [wuq3[:, :, :MLA_NOPE].reshape(MLA_Q_LORA, N_HEADS * MLA_NOPE),
         jnp.pad(wuq3[:, :, MLA_NOPE:], ((0, 0), (0, 0), (0, HEAD_W - MLA_ROPE))
                 ).reshape(MLA_Q_LORA, N_HEADS * HEAD_W)], axis=1).astype(BF16)
    wukv = mla_w_ukv[0].astype(BF16)
    wout = w_out[0].astype(BF16)
    wg = ffn_w_gate[0].astype(BF16)
    wu = ffn_w_up[0].astype(BF16)
    wd = ffn_w_down[0].astype(BF16)
    g_attn = norm_attn_g[0][None]
    g_q = mla_q_norm_g[0][None]
    g_kv = mla_kv_norm_g[0][None]
    g_sub = diff_subln_g[0][None]
    g_ffn = norm_ffn_g[0][None]
    g_fin = final_norm_g[None]
    lam_p = diff_lambda[0]

    pos_meta = jnp.arange(N_META, dtype=jnp.int32)
    pos_frames = N_META + jnp.arange(seq, dtype=jnp.int32)
    pos_dec = past + jnp.arange(dseq, dtype=jnp.int32)
    bdiag_d, corner_d, bmeta_d, bias_dp, bias_dn = _bias_tiles(rel_bias, tile, past, dseq)
    ti = jnp.arange(tile, dtype=jnp.int32)
    visible = (ti[:, None] // CHUNK) <= (ti[None, :] // CHUNK)
    bdiag_m = (jnp.where(visible, 0.0, MASK_NEG).astype(F32) * LOG2E)[None]
    bmeta_m = jnp.zeros((1, 2, N_META, tile), F32)

    def proj(x, pos, tm, v_t):
        cos_t, sin_t = _rope_tables(pos)
        return _proj(x, cos_t, sin_t, g_attn, win, g_q, wuq, g_kv, wukv, tm, v_t)

    (qd_p, kd_p, vd_p, ckv_p, kr_p, kdb_p, vdt_p, qm_p, kmr_p, vmt_p) = proj(x_prompt, pos_frames, PROJ_TM, True)
    dec_rows = ndec * dseq
    (qd_s, kd_s, vd_s, ckv_s, kr_s, kdb_s, vdb_s, qm_s, kmr_s, vm_s) = proj(
        x_sample.reshape(1, dec_rows, D_MODEL), jnp.tile(pos_dec, ndec), min(PROJ_TM, dec_rows), False)
    (_, kd_m, vd_m, ckv_m, kr_m, kdb_m, vdb_m, _, kmr_m, vm_m) = proj(meta_tokens[None], pos_meta, N_META, False)

    ones_rows = jnp.zeros((1, N_HEADS, V_ROWS - HEAD_W, N_META), BF16).at[:, :, 0, :].set(1.0)

    def meta_vt(v):
        return jnp.concatenate([jnp.swapaxes(v, 2, 3), ones_rows], axis=2)

    mix_d = _flash(qd_p, kdb_p, vdt_p, kdb_m, meta_vt(vdb_m), bmeta_d, bdiag_d,
                   (corner_d, lam_p, g_sub), tile=tile, diff=True)
    pair = lambda a: a.reshape(a.shape[0], N_HEADS // MLA_HEADS_PER_STEP, MLA_HEADS_PER_STEP, *a.shape[2:])
    mix_m = _flash(pair(qm_p), pair(kmr_p), pair(vmt_p), pair(kmr_m), pair(meta_vt(vm_m)), bmeta_m,
                   bdiag_m, None, tile=tile, diff=False)

    pk = cache_diff_k[0].reshape(ndec, past, D_DIFF_QK)
    pv = cache_diff_v[0].reshape(ndec, past * N_HEADS, HEAD_W)
    pkr_pad = jnp.pad(cache_mla_krope[0], ((0, 0), (0, 0), (0, HEAD_W - MLA_ROPE)))
    mixs_d = _attn_dec_diff(qd_s, pk, pv, kdb_s, vdb_s, bias_dp, bias_dn, lam_p, g_sub, dseq)
    mixs_m = _attn_dec_mla(qm_s, cache_mla_ckv[0], pkr_pad, wukv, kmr_s, vm_s, dseq)

    def outffn(x, md, mm):
        rows = x.shape[0] * x.shape[1]
        y = _outffn(x.reshape(rows, D_MODEL), md.reshape(rows, D_DIFF), mm.reshape(rows, D_DIFF),
                    wout, g_ffn, wg, wu, wd, g_fin, min(FFN_TM, rows), FFN_TF)
        return y.reshape(x.shape)

    y_prompt = outffn(x_prompt, mix_d, mix_m)
    y_sample = outffn(x_sample, mixs_d, mixs_m)

    def with_meta(meta_rows, frame_rows):
        meta_b = jnp.broadcast_to(meta_rows, (nb,) + meta_rows.shape[1:])
        return jnp.concatenate([meta_b, frame_rows], axis=1)[None]

    length = N_META + seq
    return (
        y_prompt, y_sample,
        with_meta(kd_m, kd_p).reshape(1, nb, length, N_HEADS, 2, DIFF_HEAD_DIM),
        with_meta(vd_m, vd_p).reshape(1, nb, length, N_HEADS, HEAD_W),
        with_meta(ckv_m, ckv_p), with_meta(kr_m, kr_p),
        kd_s.reshape(1, ndec, dseq, N_HEADS, 2, DIFF_HEAD_DIM),
        vd_s.reshape(1, ndec, dseq, N_HEADS, HEAD_W),
        ckv_s.reshape(1, ndec, dseq, MLA_KV_LORA), kr_s.reshape(1, ndec, dseq, MLA_ROPE),
    )
```

```python
import functools
import math

import jax
import jax.numpy as jnp
from jax import lax
from jax.experimental import pallas as pl
from jax.experimental.pallas import tpu as pltpu

F32 = jnp.float32
BF16 = jnp.bfloat16

D_MODEL = 2048
CHUNK = 64
N_META = 16
EPS = 1e-6
N_HEADS = 8
DIFF_HEAD_DIM = 64
HEAD_W = 128
D_DIFF_QK = N_HEADS * 2 * DIFF_HEAD_DIM
D_DIFF = N_HEADS * HEAD_W
MLA_NOPE = 128
MLA_ROPE = 64
MLA_QK_PAD = 256
MLA_Q_LORA = 512
MLA_KV_LORA = 256
DIFF_SCALE = DIFF_HEAD_DIM ** -0.5
MLA_SCALE = (MLA_NOPE + MLA_ROPE) ** -0.5
ROPE_THETA = 10000.0
REL_BUCKETS = 32
REL_MAX_DIST = 128
OFF_K = D_DIFF_QK
OFF_V = 2 * D_DIFF_QK
OFF_CQ = OFF_V + D_DIFF
OFF_CKV = OFF_CQ + MLA_Q_LORA
OFF_KR = OFF_CKV + MLA_KV_LORA
D_IN = OFF_KR + MLA_ROPE
D_IN_PAD = OFF_KR + HEAD_W
LAM_INIT = 0.8 - 0.6 * math.exp(-0.3 * 0)
MASK_NEG = -1e30
LOG2E = math.log2(math.e)
Q_DIFF_SCALE = DIFF_SCALE * LOG2E
Q_MLA_SCALE = MLA_SCALE * LOG2E
V_ROWS = HEAD_W + 16

LANES = 128
MXU_COLS = 256
VMEM_LIMIT_BYTES = 58 * 1024 * 1024

PROJ_TM = 256
FLASH_TILE = 512
FIRST_SLOT = 2
MLA_HEADS_PER_STEP = 2
FFN_TM = 512
FFN_TF = 512


def _compiler_params(semantics):
    return pltpu.CompilerParams(dimension_semantics=semantics,
                                vmem_limit_bytes=VMEM_LIMIT_BYTES)


def _resident_spec(shape):
    nd = len(shape)
    return pl.BlockSpec(shape, lambda *_: (0,) * nd, pipeline_mode=pl.Buffered(1))


def _rms(x, g):
    ms = jnp.mean(x * x, axis=-1, keepdims=True)
    return x * lax.rsqrt(ms + EPS) * g


def _dot(a, b):
    return jnp.dot(a, b, preferred_element_type=F32)


def _dot_nt(a, b):
    return lax.dot_general(a, b, (((1,), (1,)), ((), ())), preferred_element_type=F32)


def _rope128(r, cos_t, sin_t):
    lane = lax.broadcasted_iota(jnp.int32, r.shape, 1)
    first_half = (lane % MLA_ROPE) < (MLA_ROPE // 2)
    swapped = jnp.where(first_half,
                        pltpu.roll(r, LANES - MLA_ROPE // 2, 1),
                        pltpu.roll(r, MLA_ROPE // 2, 1))
    return r * cos_t + swapped * sin_t


def _ones_rows(width):
    row = lax.broadcasted_iota(jnp.int32, (V_ROWS - HEAD_W, width), 0)
    return jnp.where(row == 0, 1.0, 0.0)


def _store_head_v(ref, head, z, v_t):
    if v_t:
        ref[0, head, 0] = jnp.concatenate([z.T, _ones_rows(z.shape[0])], axis=0).astype(BF16)
    else:
        ref[0, head] = z.astype(BF16)


def _proj_kernel(x_ref, cos_ref, sin_ref, g_ref, win_ref, gq_ref, wuq_ref, gkv_ref, wukv_ref,
                 qd_ref, kd_ref, vd_ref, ckv_ref, kr_ref, kdb_ref, vdb_ref, qm_ref, kmr_ref, vm_ref,
                 *, v_t):
    h = _rms(x_ref[0], g_ref[...]).astype(BF16)
    lane = lax.broadcasted_iota(jnp.int32, (h.shape[0], HEAD_W), 1)
    low = lane < DIFF_HEAD_DIM
    cw = 4 * HEAD_W
    chunks = [(j, hh, j * (cw // HEAD_W) + hh, slice(hh * HEAD_W, (hh + 1) * HEAD_W))
              for j in range(D_DIFF_QK // cw) for hh in range(cw // HEAD_W)]

    def value_columns(j):
        zv = _dot(h, win_ref[:, OFF_V + j * cw:OFF_V + (j + 1) * cw])
        vd_ref[0, :, j * cw:(j + 1) * cw] = zv
        for jj, hh, head, sl in chunks:
            if jj == j:
                _store_head_v(vdb_ref, head, zv[:, sl], v_t)

    cos_t = cos_ref[...]
    sin_t = sin_ref[...]
    zcq = _dot(h, win_ref[:, OFF_CQ:OFF_CKV])
    zc = _dot(h, win_ref[:, OFF_CKV:D_IN_PAD])
    value_columns(0)
    cq = _rms(zcq, gq_ref[...]).astype(BF16)
    ckv = _rms(zc[:, :MLA_KV_LORA], gkv_ref[...])
    ckv_ref[0] = ckv
    kr = _rope128(zc[:, MLA_KV_LORA:], cos_t, sin_t)
    kr_ref[0] = kr[:, :MLA_ROPE]
    krb = kr.astype(BF16)
    qm = _dot(cq, wuq_ref[...])
    kv = _dot(ckv.astype(BF16), wukv_ref[...])
    for head in range(N_HEADS):
        qm_ref[0, head, :, :MLA_NOPE] = (qm[:, head * HEAD_W:(head + 1) * HEAD_W] * Q_MLA_SCALE).astype(BF16)
        qr = qm[:, D_DIFF + head * HEAD_W:D_DIFF + (head + 1) * HEAD_W]
        qm_ref[0, head, :, MLA_NOPE:] = (_rope128(qr, cos_t, sin_t) * Q_MLA_SCALE).astype(BF16)
        kmr_ref[0, head, :, :MLA_NOPE] = kv[:, head * 2 * HEAD_W:head * 2 * HEAD_W + HEAD_W].astype(BF16)
        kmr_ref[0, head, :, MLA_NOPE:] = krb
        _store_head_v(vm_ref, head, kv[:, head * 2 * HEAD_W + HEAD_W:(head + 1) * 2 * HEAD_W], v_t)

    for j in range(1, D_DIFF // cw):
        value_columns(j)
    for j in range(D_DIFF_QK // cw):
        zk = _dot(h, win_ref[:, OFF_K + j * cw:OFF_K + (j + 1) * cw])
        kd_ref[0, :, j * cw:(j + 1) * cw] = zk
        for jj, hh, head, sl in chunks:
            if jj == j:
                kdb_ref[0, head] = zk[:, sl].astype(BF16)
    for j in range(D_DIFF_QK // cw):
        zq = _dot(h, win_ref[:, j * cw:(j + 1) * cw]) * Q_DIFF_SCALE
        for jj, hh, head, sl in chunks:
            if jj == j:
                qd_ref[0, head, 0] = jnp.where(low, zq[:, sl], 0.0).astype(BF16)
                qd_ref[0, head, 1] = jnp.where(low, 0.0, zq[:, sl]).astype(BF16)


def _proj(x, cos_t, sin_t, g, win, gq, wuq, gkv, wukv, tm, v_t):
    nb, length, _ = x.shape
    assert length % tm == 0
    grid = (nb, length // tm)
    row = lambda b, i: (b, i, 0)
    headrow = lambda b, i: (b, 0, i, 0)
    if v_t:
        v_shape = jax.ShapeDtypeStruct((nb, N_HEADS, length // tm, V_ROWS, tm), BF16)
        v_spec = pl.BlockSpec((1, N_HEADS, 1, V_ROWS, tm), lambda b, i: (b, 0, i, 0, 0))
    else:
        v_shape = jax.ShapeDtypeStruct((nb, N_HEADS, length, HEAD_W), BF16)
        v_spec = pl.BlockSpec((1, N_HEADS, tm, HEAD_W), headrow)
    out_shape = (
        jax.ShapeDtypeStruct((nb, N_HEADS, 2, length, HEAD_W), BF16),
        jax.ShapeDtypeStruct((nb, length, D_DIFF_QK), F32),
        jax.ShapeDtypeStruct((nb, length, D_DIFF), F32),
        jax.ShapeDtypeStruct((nb, length, MLA_KV_LORA), F32),
        jax.ShapeDtypeStruct((nb, length, MLA_ROPE), F32),
        jax.ShapeDtypeStruct((nb, N_HEADS, length, HEAD_W), BF16),
        v_shape,
        jax.ShapeDtypeStruct((nb, N_HEADS, length, MLA_QK_PAD), BF16),
        jax.ShapeDtypeStruct((nb, N_HEADS, length, MLA_QK_PAD), BF16),
        v_shape,
    )
    out_specs = (
        pl.BlockSpec((1, N_HEADS, 2, tm, HEAD_W), lambda b, i: (b, 0, 0, i, 0)),
        pl.BlockSpec((1, tm, D_DIFF_QK), row),
        pl.BlockSpec((1, tm, D_DIFF), row),
        pl.BlockSpec((1, tm, MLA_KV_LORA), row),
        pl.BlockSpec((1, tm, MLA_ROPE), row),
        pl.BlockSpec((1, N_HEADS, tm, HEAD_W), headrow),
        v_spec,
        pl.BlockSpec((1, N_HEADS, tm, MLA_QK_PAD), headrow),
        pl.BlockSpec((1, N_HEADS, tm, MLA_QK_PAD), headrow),
        v_spec,
    )
    in_specs = [
        pl.BlockSpec((1, tm, D_MODEL), row),
        pl.BlockSpec((tm, LANES), lambda b, i: (i, 0)),
        pl.BlockSpec((tm, LANES), lambda b, i: (i, 0)),
        _resident_spec(g.shape), _resident_spec(win.shape), _resident_spec(gq.shape),
        _resident_spec(wuq.shape), _resident_spec(gkv.shape), _resident_spec(wukv.shape),
    ]
    return pl.pallas_call(
        functools.partial(_proj_kernel, v_t=v_t), grid=grid, in_specs=in_specs,
        out_specs=out_specs, out_shape=out_shape,
        compiler_params=_compiler_params(("parallel", "parallel")), name="proj",
    )(x, cos_t, sin_t, g, win, gq, wuq, gkv, wukv)


def _lambda_full(lam_ref):
    lp = lam_ref[...]
    a = jnp.sum(lp[0:1] * lp[1:2], axis=-1, keepdims=True)
    b = jnp.sum(lp[2:3] * lp[3:4], axis=-1, keepdims=True)
    return jnp.exp(a) - jnp.exp(b) + LAM_INIT


def _add_tile(s, ncomp, tile):
    rows, cols = tile.shape
    return (s.reshape(ncomp, rows, cols) + tile[None]).reshape(ncomp * rows, cols)


def _flash_kernel(*refs, tile, ncomp, diff):
    if diff:
        (q_ref, k_ref, vt_ref, kmeta_ref, vtmeta_ref, bmeta_ref, bdiag_ref, corner_ref, lam_ref, g_ref,
         o_ref, s_sc, mx_sc, m_sc, acc_sc, fin_sc) = refs
    else:
        (q_ref, k_ref, vt_ref, kmeta_ref, vtmeta_ref, bmeta_ref, bdiag_ref,
         o_ref, s_sc, mx_sc, m_sc, acc_sc, fin_sc) = refs
    shared_kv = len(k_ref.shape) == 4
    seq = k_ref.shape[-2]
    nq = seq // tile
    qcols = ncomp * tile
    ngroups = 2
    gw = qcols // ngroups
    kblock = vt_ref.shape[-1]
    kb_per_tile = tile // kblock
    n_meta = kmeta_ref.shape[-2]

    def of_comp(ref, comp):
        return ref.at[0, 0] if shared_kv else ref.at[0, 0, comp]

    def q_group(q0, g):
        comp, off = divmod(g * gw, tile)
        return q_ref[0, 0, comp, pl.ds(q0 + off, gw), :]

    def group_cols(g):
        return slice(g * gw, (g + 1) * gw)

    def bias_cols(g):
        start = (g * gw) % tile
        return slice(start, start + gw)

    def diagonal_logits_stage(q0, kj, slot):
        k0 = pl.multiple_of(kj * tile, tile)
        nsub = MXU_COLS // LANES
        for cp in range(qcols // MXU_COLS):
            comp, off = divmod(cp * MXU_COLS, tile)
            nk = off + MXU_COLS
            qg = q_ref[0, 0, comp, pl.ds(q0 + off, MXU_COLS), :]
            k = of_comp(k_ref, comp)[pl.ds(k0, nk), :]
            t = _dot_nt(k, qg) + bdiag_ref[0, 0:nk, off:off + MXU_COLS]
            for c in range(nsub):
                s_sc[slot, cp * nsub + c, 0:nk, :] = t[:, c * LANES:(c + 1) * LANES]
            mx_sc[slot, 0:1, cp * MXU_COLS:(cp + 1) * MXU_COLS] = jnp.max(t, axis=0, keepdims=True)

    def logits_stage(q0, kj, near, slot):
        k0 = pl.multiple_of(kj * tile, tile)
        for g in range(ngroups):
            k = of_comp(k_ref, (g * gw) // tile)[pl.ds(k0, tile), :]
            t = _dot_nt(k, q_group(q0, g))
            blocks = [t[:, c * LANES:(c + 1) * LANES] for c in range(gw // LANES)]
            maxima = [jnp.max(b, axis=0, keepdims=True) for b in blocks]
            base = g * (gw // LANES)
            first = 0
            if diff and (g * gw) % tile == 0:
                top = blocks[0][:tile - LANES]
                corner = blocks[0][tile - LANES:] + corner_ref[0, near]
                s_sc[slot, base, 0:tile - LANES, :] = top
                s_sc[slot, base, tile - LANES:tile, :] = corner
                maxima[0] = jnp.maximum(jnp.max(top, axis=0, keepdims=True),
                                        jnp.max(corner, axis=0, keepdims=True))
                first = 1
            for c in range(first, gw // LANES):
                s_sc[slot, base + c, 0:tile, :] = blocks[c]
            mx_sc[slot, 0:1, group_cols(g)] = jnp.concatenate(maxima, axis=1)

    def meta_logits_stage(q0, mkind, slot):
        for g in range(ngroups):
            gc = group_cols(g)
            km = of_comp(kmeta_ref, (g * gw) // tile)[...]
            t = _dot_nt(km, q_group(q0, g)) + bmeta_ref[0, mkind, :, bias_cols(g)]
            for c in range(gw // LANES):
                s_sc[slot, g * (gw // LANES) + c, tile:tile + n_meta, :] = t[:, c * LANES:(c + 1) * LANES]
            mx_sc[slot, 0:1, gc] = jnp.maximum(mx_sc[slot, 0:1, gc], jnp.max(t, axis=0, keepdims=True))

    def probs_stage(kj, slot, diagonal):
        nsub = MXU_COLS // LANES
        for cp in range(qcols // MXU_COLS):
            comp = (cp * MXU_COLS) // tile
            vts = [(of_comp(vt_ref, comp)[kj * kb_per_tile + j], j * kblock) for j in range(kb_per_tile)]
            if diagonal:
                vts.append((of_comp(vtmeta_ref, comp)[...], tile))
            cols = slice(cp * MXU_COLS, (cp + 1) * MXU_COLS)
            last_query = (cp * MXU_COLS) % tile + MXU_COLS - 1
            m_prev = m_sc[0:1, cols]
            m_new = jnp.maximum(m_prev, mx_sc[slot, 0:1, cols])
            m_sc[0:1, cols] = m_new
            pv = None
            for vt, r0 in vts:
                nk = vt.shape[1]
                if diagonal and r0 < tile and r0 // CHUNK > last_query // CHUNK:
                    continue
                pieces = []
                for c in range(nsub):
                    first_query = (cp * MXU_COLS) % tile + c * LANES
                    seen = nk
                    if diagonal and r0 < tile:
                        seen = min(nk, max(0, (first_query + LANES - 1) // CHUNK * CHUNK + CHUNK - r0))
                    piece = jnp.exp2(s_sc[slot, cp * nsub + c, r0:r0 + seen, :]
                                     - m_new[:, c * LANES:(c + 1) * LANES])
                    if seen < nk:
                        piece = jnp.concatenate([piece, jnp.zeros((nk - seen, LANES), F32)], axis=0)
                    pieces.append(piece)
                p = jnp.concatenate(pieces, axis=1).astype(BF16)
                part = _dot(vt, p)
                pv = part if pv is None else pv + part
            acc_sc[:, cols] = jnp.exp2(m_prev - m_new) * acc_sc[:, cols] + pv

    def write_out(q_start):
        o_t = fin_sc[0:HEAD_W, :] * (1.0 / fin_sc[HEAD_W:HEAD_W + 1, :])
        if diff:
            od = (o_t[:, :tile] - _lambda_full(lam_ref) * o_t[:, tile:]).T
            o = _rms(od, g_ref[...]) * (1.0 - LAM_INIT)
        else:
            o = jnp.concatenate([o_t[:, c * tile:(c + 1) * tile].T for c in range(ncomp)], axis=1)
        o_ref[0, pl.ds(q_start, tile), :] = o.astype(o_ref.dtype)

    def q_body(qi, carry):
        q0 = pl.multiple_of(qi * tile, tile)

        def near(t, q_index):
            return jnp.clip(t - q_index + 2, 0, 1)

        def step(t, slot, next_slot, next_is_diagonal):
            if next_is_diagonal:
                diagonal_logits_stage(q0, t + 1, next_slot)
            else:
                logits_stage(q0, t + 1, near(t + 1, qi), next_slot)
            probs_stage(t, slot, False)

        def next_query_tile_logits():
            qn = jnp.minimum(qi + 1, nq - 1)
            logits_stage(pl.multiple_of(qn * tile, tile), 0, near(0, qn), FIRST_SLOT)

        def last_step(slot):
            meta_logits_stage(q0, jnp.minimum(qi, 1), slot)
            if slot != FIRST_SLOT:
                next_query_tile_logits()
            probs_stage(qi, slot, True)
            if slot == FIRST_SLOT:
                next_query_tile_logits()
            fin_sc[...] = acc_sc[...]

        m_sc[0:1, :] = jnp.full((1, qcols), MASK_NEG, F32)
        acc_sc[...] = jnp.zeros(acc_sc.shape, F32)

        @pl.when(qi == 1)
        def _():
            step(0, FIRST_SLOT, 0, True)
            write_out(q0 - tile)

        @pl.when(qi >= 2)
        def _():
            step(0, FIRST_SLOT, 0, False)
            write_out(q0 - tile)

        def pair(u, c):
            step(2 * u + 1, 0, 1, False)
            step(2 * u + 2, 1, 0, False)
            return c

        lax.fori_loop(0, (qi - 2) // 2, pair, 0)

        @pl.when((qi % 2 == 0) & (qi >= 2))
        def _():
            step(qi - 1, 0, 1, True)
            last_step(1)

        @pl.when((qi % 2 == 1) & (qi >= 3))
        def _():
            step(qi - 2, 0, 1, False)
            step(qi - 1, 1, 0, True)
            last_step(0)

        @pl.when(qi == 1)
        def _():
            last_step(0)

        @pl.when(qi == 0)
        def _():
            last_step(FIRST_SLOT)

        return carry

    assert nq >= 2
    diagonal_logits_stage(0, 0, FIRST_SLOT)
    lax.fori_loop(0, nq, q_body, 0)
    write_out((nq - 1) * tile)


def _flash(q, k, vt, kmeta, vtmeta, bmeta, bdiag, extra, *, tile, diff):
    nb, nh, ncomp, seq, dk = q.shape
    qcols = ncomp * tile
    out_w = HEAD_W if diff else ncomp * HEAD_W
    per_head = bdiag.shape[0] > 1

    def bh(nd):
        return lambda h, b: ((h if per_head else 0),) + (0,) * (nd - 1)

    def per_group(a, batched):
        nd = a.ndim
        return pl.BlockSpec((1, 1) + a.shape[2:], lambda h, b: ((b if batched else 0), h) + (0,) * (nd - 2))

    in_specs = [
        per_group(q, True), per_group(k, True), per_group(vt, True),
        per_group(kmeta, False), per_group(vtmeta, False),
        pl.BlockSpec((1,) + bmeta.shape[1:], bh(4)),
        pl.BlockSpec((1,) + bdiag.shape[1:], bh(3)),
    ]
    args = [q, k, vt, kmeta, vtmeta, bmeta, bdiag]
    if diff:
        corner, lam_p, g = extra
        in_specs += [pl.BlockSpec((1,) + corner.shape[1:], bh(4)),
                     pl.BlockSpec(lam_p.shape, lambda h, b: (0, 0)),
                     pl.BlockSpec(g.shape, lambda h, b: (0, 0))]
        args += [corner, lam_p, g]
    return pl.pallas_call(
        functools.partial(_flash_kernel, tile=tile, ncomp=ncomp, diff=diff),
        grid=(nh, nb), in_specs=in_specs,
        out_specs=pl.BlockSpec((1, seq, out_w), lambda h, b: (b, 0, h)),
        out_shape=jax.ShapeDtypeStruct((nb, seq, nh * out_w), BF16),
        scratch_shapes=[pltpu.VMEM((3, qcols // LANES, tile + kmeta.shape[-2], LANES), F32),
                        pltpu.VMEM((3, 8, qcols), F32),
                        pltpu.VMEM((8, qcols), F32),
                        pltpu.VMEM((V_ROWS, qcols), F32),
                        pltpu.VMEM((V_ROWS, qcols), F32)],
        compiler_params=_compiler_params(("parallel", "parallel")),
        name="flash_diff" if diff else "flash_mla",
    )(*args)


def _dec_scores(q, kp, kn, bias_p, bias_n, ncomp):
    sp = _dot_nt(q, kp)
    sn = _dot_nt(q, kn)
    if bias_p is not None:
        sp = _add_tile(sp, ncomp, bias_p)
        sn = _add_tile(sn, ncomp, bias_n)
    return sp, sn


def _dec_output(scores, vp, vn):
    sp, sn = scores
    m = jnp.maximum(jnp.max(sp, axis=-1, keepdims=True), jnp.max(sn, axis=-1, keepdims=True))
    pp = jnp.exp2(sp - m)
    pn = jnp.exp2(sn - m)
    l = jnp.sum(pp, axis=-1, keepdims=True) + jnp.sum(pn, axis=-1, keepdims=True)
    return (_dot(pp.astype(vp.dtype), vp) + _dot(pn.astype(BF16), vn)) / l


def _pipelined_heads(scores_of, finish):
    ahead = scores_of(0)
    for head in range(N_HEADS):
        current = ahead
        if head + 1 < N_HEADS:
            ahead = scores_of(head + 1)
        finish(head, current)


def _attn_dec_diff_kernel(q_ref, pk_ref, pv_ref, kn_ref, vn_ref, bp_ref, bn_ref, lam_ref, g_ref, o_ref):
    nrow = q_ref.shape[3]
    past = pk_ref.shape[1]
    lam = _lambda_full(lam_ref)

    def scores_of(head):
        sl = slice(head * HEAD_W, (head + 1) * HEAD_W)
        q = q_ref[0, head].reshape(2 * nrow, HEAD_W)
        return _dec_scores(q, pk_ref[0, :, sl].astype(BF16), kn_ref[0, head], bp_ref[head], bn_ref[head], 2)

    def finish(head, scores):
        o = _dec_output(scores, pv_ref[0, pl.ds(head, past, stride=N_HEADS), :], vn_ref[0, head])
        od = o[:nrow] - lam * o[nrow:]
        o_ref[0, :, head * HEAD_W:(head + 1) * HEAD_W] = (
            _rms(od, g_ref[...]) * (1.0 - LAM_INIT)).astype(o_ref.dtype)

    _pipelined_heads(scores_of, finish)


def _attn_dec_mla_kernel(q_ref, pckv_ref, pkr_ref, wukv_ref, kn_ref, vn_ref, o_ref):
    ckv = pckv_ref[0].astype(BF16)
    krb = pkr_ref[0].astype(BF16)

    def scores_of(head):
        kv = _dot(ckv, wukv_ref[:, head * 2 * HEAD_W:(head + 1) * 2 * HEAD_W])
        kp = jnp.concatenate([kv[:, :HEAD_W].astype(BF16), krb], axis=1)
        scores = _dec_scores(q_ref[0, head], kp, kn_ref[0, head], None, None, 1)
        return scores, kv[:, HEAD_W:].astype(BF16)

    def finish(head, scores_and_v):
        scores, vp = scores_and_v
        o = _dec_output(scores, vp, vn_ref[0, head])
        o_ref[0, :, head * HEAD_W:(head + 1) * HEAD_W] = o.astype(o_ref.dtype)

    _pipelined_heads(scores_of, finish)


def _attn_dec_diff(q, pk, pv, kn, vn, bias_p, bias_n, lam_p, g, nrow):
    _, nh, ncomp, _, dk = q.shape
    nb, past = pk.shape[:2]
    batch = lambda nd: (lambda b: (b,) + (0,) * (nd - 1))
    rows = lambda b: (0, 0, b, 0)
    return pl.pallas_call(
        _attn_dec_diff_kernel, grid=(nb,),
        in_specs=[pl.BlockSpec((1, nh, ncomp, nrow, dk), lambda b: (0, 0, 0, b, 0)),
                  pl.BlockSpec((1, past, D_DIFF_QK), batch(3)),
                  pl.BlockSpec((1, past * nh, HEAD_W), batch(3)),
                  pl.BlockSpec((1, nh, nrow, dk), rows), pl.BlockSpec((1, nh, nrow, HEAD_W), rows),
                  _resident_spec(bias_p.shape), _resident_spec(bias_n.shape),
                  _resident_spec(lam_p.shape), _resident_spec(g.shape)],
        out_specs=pl.BlockSpec((1, nrow, D_DIFF), batch(3)),
        out_shape=jax.ShapeDtypeStruct((nb, nrow, D_DIFF), BF16),
        compiler_params=_compiler_params(("parallel",)), name="attn_dec_diff",
    )(q, pk, pv, kn, vn, bias_p, bias_n, lam_p, g)


def _attn_dec_mla(q, pckv, pkr_pad, wukv, kn, vn, nrow):
    _, nh, _, dk = q.shape
    nb, past = pckv.shape[:2]
    batch = lambda nd: (lambda b: (b,) + (0,) * (nd - 1))
    rows = lambda b: (0, 0, b, 0)
    return pl.pallas_call(
        _attn_dec_mla_kernel, grid=(nb,),
        in_specs=[pl.BlockSpec((1, nh, nrow, dk), rows),
                  pl.BlockSpec((1, past, MLA_KV_LORA), batch(3)), pl.BlockSpec((1, past, HEAD_W), batch(3)),
                  _resident_spec(wukv.shape),
                  pl.BlockSpec((1, nh, nrow, dk), rows), pl.BlockSpec((1, nh, nrow, HEAD_W), rows)],
        out_specs=pl.BlockSpec((1, nrow, D_DIFF), batch(3)),
        out_shape=jax.ShapeDtypeStruct((nb, nrow, D_DIFF), BF16),
        compiler_params=_compiler_params(("parallel",)), name="attn_dec_mla",
    )(q, pckv, pkr_pad, wukv, kn, vn)


def _outffn_kernel(x_ref, md_ref, mm_ref, wout_ref, gffn_ref, wg_ref, wu_ref, wd_ref, gfin_ref,
                   o_ref, h_sc, acc_sc):
    f = pl.program_id(1)

    @pl.when(f == 0)
    def _():
        x1 = (x_ref[...] + _dot(md_ref[...], wout_ref[:D_DIFF, :])
              + _dot(mm_ref[...], wout_ref[D_DIFF:, :]))
        acc_sc[...] = x1
        h_sc[...] = _rms(x1, gffn_ref[...]).astype(BF16)

    h = h_sc[...]
    gate = _dot(h, wg_ref[...])
    up = _dot(h, wu_ref[...])
    act = (gate * jax.nn.sigmoid(gate) * up).astype(BF16)
    acc_sc[...] += _dot(act, wd_ref[...])

    @pl.when(f == pl.num_programs(1) - 1)
    def _():
        o_ref[...] = _rms(acc_sc[...], gfin_ref[...])


def _outffn(x, mix_d, mix_m, wout, gffn, wg, wu, wd, gfin, tm, tf):
    m = x.shape[0]
    d_ff = wg.shape[1]
    assert m % tm == 0 and d_ff % tf == 0
    row = lambda i, f: (i, 0)
    return pl.pallas_call(
        _outffn_kernel, grid=(m // tm, d_ff // tf),
        in_specs=[pl.BlockSpec((tm, D_MODEL), row), pl.BlockSpec((tm, D_DIFF), row),
                  pl.BlockSpec((tm, D_DIFF), row), _resident_spec(wout.shape),
                  _resident_spec(gffn.shape),
                  pl.BlockSpec((D_MODEL, tf), lambda i, f: (0, f)),
                  pl.BlockSpec((D_MODEL, tf), lambda i, f: (0, f)),
                  pl.BlockSpec((tf, D_MODEL), lambda i, f: (f, 0)),
                  _resident_spec(gfin.shape)],
        out_specs=pl.BlockSpec((tm, D_MODEL), row),
        out_shape=jax.ShapeDtypeStruct((m, D_MODEL), F32),
        scratch_shapes=[pltpu.VMEM((tm, D_MODEL), BF16), pltpu.VMEM((tm, D_MODEL), F32)],
        compiler_params=_compiler_params(("parallel", "arbitrary")), name="outffn",
    )(x, mix_d, mix_m, wout, gffn, wg, wu, wd, gfin)


def _t5_bucket(rel):
    nb = REL_BUCKETS // 2
    max_exact = nb // 2
    ret = jnp.where(rel > 0, nb, 0)
    n = jnp.abs(rel)
    large = max_exact + (jnp.log(jnp.maximum(n, 1).astype(F32) / max_exact)
                         / math.log(REL_MAX_DIST / max_exact) * (nb - max_exact)).astype(jnp.int32)
    large = jnp.minimum(large, nb - 1)
    return ret + jnp.where(n < max_exact, n, large)


def _rope_tables(pos):
    half = MLA_ROPE // 2
    inv_freq = ROPE_THETA ** (-jnp.arange(half, dtype=F32) / half)
    ang = pos.astype(F32)[:, None] * inv_freq[None, :]
    cos, sin = jnp.cos(ang), jnp.sin(ang)
    zero = jnp.zeros_like(cos)
    return (jnp.concatenate([cos, cos, zero, zero], axis=1),
            jnp.concatenate([-sin, sin, zero, zero], axis=1))


def _bias_tiles_kernel(rb_ref, far_ref, bkt_ref, bdiag_ref, corner_ref, bmeta_ref, bdp_ref, bdn_ref,
                       *, tile, past, dseq):
    h = pl.program_id(0)
    n = bkt_ref.shape[1]

    def table(row):
        bkt = bkt_ref[row:row + 1, :]
        out = jnp.zeros((1, n), F32)
        for b in range(REL_BUCKETS):
            out = out + jnp.where(bkt == b, rb_ref[b * N_HEADS + h], 0.0)
        return out

    def toeplitz(tbl, rows, shift, cols):
        x = jnp.broadcast_to(tbl, (rows, n))
        return pltpu.roll(x, shift, 1, stride=1, stride_axis=0)[:, :cols]

    far = rb_ref[far_ref[0] * N_HEADS + h]
    tbl = (table(0) - far) * LOG2E
    masked = MASK_NEG * LOG2E

    def flash_shift(offset):
        return (2 * tile + offset + 1) % n

    kk = lax.broadcasted_iota(jnp.int32, (tile, tile), 0)
    qq = lax.broadcasted_iota(jnp.int32, (tile, tile), 1)
    visible = (kk // CHUNK) <= (qq // CHUNK)
    bdiag_ref[0] = jnp.where(visible, toeplitz(tbl, tile, flash_shift(0), tile), masked)
    corner_ref[0, 0] = jnp.zeros((LANES, LANES), F32)
    corner_ref[0, 1] = toeplitz(tbl, LANES, flash_shift(-LANES), LANES)
    bmeta_ref[0, 0] = toeplitz(tbl, N_META, flash_shift(-N_META), tile)
    bmeta_ref[0, 1] = jnp.zeros((N_META, tile), F32)

    tbl_d = table(1) * LOG2E
    off = past + dseq - 1
    bdp_ref[0] = toeplitz(tbl_d, dseq, (n - dseq + 1) % n, past)
    bdn_ref[0] = toeplitz(tbl_d, dseq, (n - off) % n, dseq)


def _bias_tiles(rel_bias, tile, past, dseq):
    n = 4 * tile
    off = past + dseq - 1
    assert n >= off + dseq and n % LANES == 0
    idx = jnp.arange(n, dtype=jnp.int32)
    bkt = jnp.stack([_t5_bucket((n - 1 - idx) - 2 * tile), _t5_bucket(idx - off)])
    far = _t5_bucket(jnp.full((1,), -REL_MAX_DIST, jnp.int32))
    smem = pl.BlockSpec(memory_space=pltpu.SMEM)
    head = lambda nd: (lambda h: (h,) + (0,) * (nd - 1))
    return pl.pallas_call(
        functools.partial(_bias_tiles_kernel, tile=tile, past=past, dseq=dseq),
        grid=(N_HEADS,),
        in_specs=[smem, smem, pl.BlockSpec(bkt.shape, lambda h: (0, 0))],
        out_specs=(pl.BlockSpec((1, tile, tile), head(3)),
                   pl.BlockSpec((1, 2, LANES, LANES), head(4)),
                   pl.BlockSpec((1, 2, N_META, tile), head(4)),
                   pl.BlockSpec((1, dseq, past), head(3)),
                   pl.BlockSpec((1, dseq, dseq), head(3))),
        out_shape=(jax.ShapeDtypeStruct((N_HEADS, tile, tile), F32),
                   jax.ShapeDtypeStruct((N_HEADS, 2, LANES, LANES), F32),
                   jax.ShapeDtypeStruct((N_HEADS, 2, N_META, tile), F32),
                   jax.ShapeDtypeStruct((N_HEADS, dseq, past), F32),
                   jax.ShapeDtypeStruct((N_HEADS, dseq, dseq), F32)),
        compiler_params=_compiler_params(("parallel",)), name="bias_tiles",
    )(rel_bias.astype(F32).reshape(-1), far, bkt)


def kernel(x_prompt, x_sample, cache_diff_k, cache_diff_v, cache_mla_ckv, cache_mla_krope,
           meta_tokens, rel_bias, norm_attn_g, w_in, diff_lambda, diff_subln_g, mla_q_norm_g,
           mla_w_uq, mla_kv_norm_g, mla_w_ukv, w_out, norm_ffn_g, ffn_w_gate, ffn_w_up,
           ffn_w_down, final_norm_g):
    nb, seq, _ = x_prompt.shape
    ndec, dseq, _ = x_sample.shape
    past = cache_diff_k.shape[2]
    assert cache_diff_k.shape[0] == 1, "single layer"
    tile = FLASH_TILE
    assert seq % tile == 0 and tile % CHUNK == 0 and tile % PROJ_TM == 0
    assert tile >= REL_MAX_DIST + LANES
    assert past % CHUNK == 0 and dseq <= CHUNK

    win = jnp.pad(w_in[0], ((0, 0), (0, D_IN_PAD - D_IN))).astype(BF16)
    wuq3 = mla_w_uq[0].reshape(MLA_Q_LORA, N_HEADS, MLA_NOPE + MLA_ROPE)
    wuq = jnp.concatenate(
        [wuq3[:, :, :MLA_NOPE].reshape(MLA_Q_LORA, N_HEADS * MLA_NOPE),
         jnp.pad(wuq3[:, :, MLA_NOPE:], ((0, 0), (0, 0), (0, HEAD_W - MLA_ROPE))
                 ).reshape(MLA_Q_LORA, N_HEADS * HEAD_W)], axis=1).astype(BF16)
    wukv = mla_w_ukv[0].astype(BF16)
    wout = w_out[0].astype(BF16)
    wg = ffn_w_gate[0].astype(BF16)
    wu = ffn_w_up[0].astype(BF16)
    wd = ffn_w_down[0].astype(BF16)
    g_attn = norm_attn_g[0][None]
    g_q = mla_q_norm_g[0][None]
    g_kv = mla_kv_norm_g[0][None]
    g_sub = diff_subln_g[0][None]
    g_ffn = norm_ffn_g[0][None]
    g_fin = final_norm_g[None]
    lam_p = diff_lambda[0]

    pos_meta = jnp.arange(N_META, dtype=jnp.int32)
    pos_frames = N_META + jnp.arange(seq, dtype=jnp.int32)
    pos_dec = past + jnp.arange(dseq, dtype=jnp.int32)
    bdiag_d, corner_d, bmeta_d, bias_dp, bias_dn = _bias_tiles(rel_bias, tile, past, dseq)
    ti = jnp.arange(tile, dtype=jnp.int32)
    visible = (ti[:, None] // CHUNK) <= (ti[None, :] // CHUNK)
    bdiag_m = (jnp.where(visible, 0.0, MASK_NEG).astype(F32) * LOG2E)[None]
    bmeta_m = jnp.zeros((1, 2, N_META, tile), F32)

    def proj(x, pos, tm, v_t):
        cos_t, sin_t = _rope_tables(pos)
        return _proj(x, cos_t, sin_t, g_attn, win, g_q, wuq, g_kv, wukv, tm, v_t)

    (qd_p, kd_p, vd_p, ckv_p, kr_p, kdb_p, vdt_p, qm_p, kmr_p, vmt_p) = proj(x_prompt, pos_frames, PROJ_TM, True)
    dec_rows = ndec * dseq
    (qd_s, kd_s, vd_s, ckv_s, kr_s, kdb_s, vdb_s, qm_s, kmr_s, vm_s) = proj(
        x_sample.reshape(1, dec_rows, D_MODEL), jnp.tile(pos_dec, ndec), min(PROJ_TM, dec_rows), False)
    (_, kd_m, vd_m, ckv_m, kr_m, kdb_m, vdb_m, _, kmr_m, vm_m) = proj(meta_tokens[None], pos_meta, N_META, False)

    ones_rows = jnp.zeros((1, N_HEADS, V_ROWS - HEAD_W, N_META), BF16).at[:, :, 0, :].set(1.0)

    def meta_vt(v):
        return jnp.concatenate([jnp.swapaxes(v, 2, 3), ones_rows], axis=2)

    mix_d = _flash(qd_p, kdb_p, vdt_p, kdb_m, meta_vt(vdb_m), bmeta_d, bdiag_d,
                   (corner_d, lam_p, g_sub), tile=tile, diff=True)
    pair = lambda a: a.reshape(a.shape[0], N_HEADS // MLA_HEADS_PER_STEP, MLA_HEADS_PER_STEP, *a.shape[2:])
    mix_m = _flash(pair(qm_p), pair(kmr_p), pair(vmt_p), pair(kmr_m), pair(meta_vt(vm_m)), bmeta_m,
                   bdiag_m, None, tile=tile, diff=False)

    pk = cache_diff_k[0].reshape(ndec, past, D_DIFF_QK)
    pv = cache_diff_v[0].reshape(ndec, past * N_HEADS, HEAD_W)
    pkr_pad = jnp.pad(cache_mla_krope[0], ((0, 0), (0, 0), (0, HEAD_W - MLA_ROPE)))
    mixs_d = _attn_dec_diff(qd_s, pk, pv, kdb_s, vdb_s, bias_dp, bias_dn, lam_p, g_sub, dseq)
    mixs_m = _attn_dec_mla(qm_s, cache_mla_ckv[0], pkr_pad, wukv, kmr_s, vm_s, dseq)

    def outffn(x, md, mm):
        rows = x.shape[0] * x.shape[1]
        y = _outffn(x.reshape(rows, D_MODEL), md.reshape(rows, D_DIFF), mm.reshape(rows, D_DIFF),
                    wout, g_ffn, wg, wu, wd, g_fin, min(FFN_TM, rows), FFN_TF)
        return y.reshape(x.shape)

    y_prompt = outffn(x_prompt, mix_d, mix_m)
    y_sample = outffn(x_sample, mixs_d, mixs_m)

    def with_meta(meta_rows, frame_rows):
        meta_b = jnp.broadcast_to(meta_rows, (nb,) + meta_rows.shape[1:])
        return jnp.concatenate([meta_b, frame_rows], axis=1)[None]

    length = N_META + seq
    return (
        y_prompt, y_sample,
        with_meta(kd_m, kd_p).reshape(1, nb, length, N_HEADS, 2, DIFF_HEAD_DIM),
        with_meta(vd_m, vd_p).reshape(1, nb, length, N_HEADS, HEAD_W),
        with_meta(ckv_m, ckv_p), with_meta(kr_m, kr_p),
        kd_s.reshape(1, ndec, dseq, N_HEADS, 2, DIFF_HEAD_DIM),
        vd_s.reshape(1, ndec, dseq, N_HEADS, HEAD_W),
        ckv_s.reshape(1, ndec, dseq, MLA_KV_LORA), kr_s.reshape(1, ndec, dseq, MLA_ROPE),
    )
```
